```python
import math
import jax, jax.numpy as jnp
from jax import lax
import numpy as np

D_MODEL = 1024
BATCH = 16
SEQ = 4096
DEPTH = 1

EPS = 1e-6
ROPE_BASE = 10000.0
Q_BLOCK = 128
POOL_WINDOWS = (2, 4, 8, 16)
N_POOL_GROUPS = 4
POOL_GROUP_DIM = 128
POOL_W = N_POOL_GROUPS * POOL_GROUP_DIM
MLA_HEADS = 8
QK_NOPE = 64
QK_ROPE = 32
QK_HEAD = QK_NOPE + QK_ROPE
V_HEAD = 64
Q_LORA = 384
KV_LORA = 256
MLA_W = MLA_HEADS * V_HEAD
MEM_LEN = 256
CROSS_HEADS = 4
CROSS_HEAD_DIM = 128
CROSS_W = CROSS_HEADS * CROSS_HEAD_DIM
N_BRANCHES = 3
OFF_POOL = 0
OFF_QL = OFF_POOL + POOL_W
OFF_KVL = OFF_QL + Q_LORA
OFF_KPE = OFF_KVL + KV_LORA
OFF_XQ = OFF_KPE + QK_ROPE
OFF_GATE = OFF_XQ + CROSS_W
IN_COLS = OFF_GATE + N_BRANCHES * D_MODEL
N_EXPERTS = 32
TOP_K = 4
D_FF = D_MODEL
SWIGLU_LIMIT = 7.0
SWIGLU_ALPHA = 1.702
EXPERT_BLOCK = 512

kernel_name = "hybrid_pool_mla_xattn_moe_block"


def rms_norm(x, g):
    xf = x.astype(jnp.float32)
    y = xf * lax.rsqrt(jnp.mean(xf * xf, axis=-1, keepdims=True) + EPS)
    return (y * g.astype(jnp.float32)).astype(x.dtype)


def rope(x, positions):
    half = x.shape[-1] // 2
    inv = ROPE_BASE ** (-jnp.arange(half, dtype=jnp.float32) / half)
    ang = positions.astype(jnp.float32)[..., None] * inv
    cos = jnp.cos(ang)[:, :, None, :]
    sin = jnp.sin(ang)[:, :, None, :]
    xf = x.astype(jnp.float32)
    x1, x2 = xf[..., :half], xf[..., half:]
    out = jnp.concatenate([x1 * cos - x2 * sin, x2 * cos + x1 * sin], axis=-1)
    return out.astype(x.dtype)


def multiscale_pool(u, w_pool, pool_scale):
    B, S, _ = u.shape
    ug = u.reshape(B, S, N_POOL_GROUPS, POOL_GROUP_DIM).astype(jnp.float32)
    cs = jnp.cumsum(ug, axis=1)
    cs = jnp.concatenate([jnp.zeros_like(cs[:, :1]), cs], axis=1)
    win = jnp.array(POOL_WINDOWS, dtype=jnp.int32)
    t = jnp.arange(S, dtype=jnp.int32)
    lower = jnp.maximum(t[:, None] + 1 - win[None, :], 0)
    cnt = jnp.minimum(t[:, None] + 1, win[None, :]).astype(jnp.float32)
    cs_low = cs[:, lower, jnp.arange(N_POOL_GROUPS)[None, :], :]
    mean = (cs[:, 1:] - cs_low) / cnt[None, :, :, None]
    mixed = (mean - ug).astype(u.dtype)
    y = jnp.einsum('bsgc,gcd->bsgd', mixed, w_pool) * pool_scale
    return y.reshape(B, S, POOL_W)


def causal_block_attention(q, k, v, scale):
    B, S, H, Dk = q.shape
    nb = S // Q_BLOCK
    qb = q.reshape(B, nb, Q_BLOCK, H, Dk).transpose(1, 0, 2, 3, 4)
    kpos = jnp.arange(S)

    def one_block(args):
        qi, i = args
        s = jnp.einsum('bqhd,bkhd->bhqk', qi, k).astype(jnp.float32) * scale
        qpos = i * Q_BLOCK + jnp.arange(Q_BLOCK)
        mask = kpos[None, :] <= qpos[:, None]
        s = jnp.where(mask[None, None], s, -jnp.inf)
        p = jax.nn.softmax(s, axis=-1).astype(v.dtype)
        return jnp.einsum('bhqk,bkhd->bqhd', p, v)

    o = lax.map(one_block, (qb, jnp.arange(nb)))
    return o.transpose(1, 0, 2, 3, 4).reshape(B, S, H, v.shape[-1])


def mla_branch(c_q, c_kv, k_pe, positions, g_q_lora, w_uq, g_kv_lora, w_ukv, g_q_mla, g_k_mla):
    B, S, _ = c_q.shape
    q = (rms_norm(c_q, g_q_lora) @ w_uq).reshape(B, S, MLA_HEADS, QK_HEAD)
    kv = (rms_norm(c_kv, g_kv_lora) @ w_ukv).reshape(B, S, MLA_HEADS, QK_NOPE + V_HEAD)
    k_nope, v = kv[..., :QK_NOPE], kv[..., QK_NOPE:]
    k_pe_h = jnp.broadcast_to(k_pe[:, :, None, :], (B, S, MLA_HEADS, QK_ROPE))
    k = jnp.concatenate([k_nope, k_pe_h], axis=-1)
    q = rms_norm(q, g_q_mla)
    k = rms_norm(k, g_k_mla)
    q = jnp.concatenate([q[..., :QK_NOPE], rope(q[..., QK_NOPE:], positions)], axis=-1)
    k = jnp.concatenate([k[..., :QK_NOPE], rope(k[..., QK_NOPE:], positions)], axis=-1)
    o = causal_block_attention(q, k, v, QK_HEAD ** -0.5)
    return o.reshape(B, S, MLA_W)


def cross_branch(q_in, mem, g_mem_norm, w_mem_kv, g_q_cross, g_k_cross):
    B, S, _ = q_in.shape
    M = mem.shape[1]
    q = rms_norm(q_in.reshape(B, S, CROSS_HEADS, CROSS_HEAD_DIM), g_q_cross)
    kv = (rms_norm(mem, g_mem_norm) @ w_mem_kv).reshape(B, M, 2, CROSS_HEADS, CROSS_HEAD_DIM)
    k = rms_norm(kv[:, :, 0], g_k_cross)
    v = kv[:, :, 1]
    s = jnp.einsum('bshd,bmhd->bhsm', q, k).astype(jnp.float32) * (CROSS_HEAD_DIM ** -0.5)
    p = jax.nn.softmax(s, axis=-1).astype(v.dtype)
    o = jnp.einsum('bhsm,bmhd->bshd', p, v)
    return o.reshape(B, S, CROSS_W)


def moe_ffn(h, w_router, b_router, w_exp_in, b_exp_in, w_exp_out, b_exp_out):
    B, S, D = h.shape
    t = h.reshape(-1, D)
    T = t.shape[0]
    N = T * TOP_K
    R = EXPERT_BLOCK
    NB = (N + N_EXPERTS * (R - 1) + R - 1) // R
    logits = (t @ w_router + b_router).astype(jnp.float32)
    top_val, top_idx = lax.top_k(logits, TOP_K)
    gate = jax.nn.softmax(top_val, axis=-1)
    flat_e = top_idx.reshape(-1).astype(jnp.int32)
    order = jnp.argsort(flat_e)
    sorted_e = flat_e[order]
    counts = jnp.bincount(flat_e, length=N_EXPERTS).astype(jnp.int32)
    starts = jnp.cumsum(counts) - counts
    padded = (counts + R - 1) // R * R
    pend = jnp.cumsum(padded)
    pstart = pend - padded
    dest_sorted = (pstart[sorted_e] + jnp.arange(N, dtype=jnp.int32) - starts[sorted_e]).astype(jnp.int32)
    dest = jnp.zeros((N,), jnp.int32).at[order].set(dest_sorted)
    buf_tok = jnp.zeros((NB * R,), jnp.int32).at[dest].set(jnp.arange(N, dtype=jnp.int32) // TOP_K)
    block_e = jnp.minimum(jnp.searchsorted(pend, jnp.arange(NB, dtype=jnp.int32) * R, side='right'),
                          N_EXPERTS - 1).astype(jnp.int32)

    def expert_block(args):
        rows, e = args
        xin = t[rows]
        hm = xin @ w_exp_in[e] + b_exp_in[e]
        glu, lin = hm[:, :D_FF], hm[:, D_FF:]
        glu = jnp.minimum(glu, SWIGLU_LIMIT)
        lin = jnp.clip(lin, -SWIGLU_LIMIT, SWIGLU_LIMIT)
        act = glu * jax.nn.sigmoid(SWIGLU_ALPHA * glu) * (lin + 1)
        return act @ w_exp_out[e] + b_exp_out[e]

    out_buf = lax.map(expert_block, (buf_tok.reshape(NB, R), block_e)).reshape(NB * R, D)
    picked = out_buf[dest].reshape(T, TOP_K, D)
    y = jnp.einsum('tk,tkd->td', gate.astype(picked.dtype), picked)
    return y.reshape(B, S, D)


def setup_inputs(seed: int = 0) -> dict:
    key = jax.random.key(seed)
    ks = jax.random.split(key, 32)
    f32 = jnp.float32

    def w(k, shape, fan_in):
        return jax.random.normal(k, shape, f32) * (fan_in ** -0.5)

    def gain(k, shape):
        return 1.0 + 0.05 * jax.random.normal(k, shape, f32)

    offs = jax.random.randint(ks[2], (BATCH, 1), 0, 1024, dtype=jnp.int32)
    positions = offs + jnp.arange(SEQ, dtype=jnp.int32)[None, :]
    return {
        "x": jax.random.normal(ks[0], (BATCH, SEQ, D_MODEL), f32),
        "mem": jax.random.normal(ks[1], (BATCH, MEM_LEN, D_MODEL), f32),
        "positions": positions,
        "g_attn_norm": gain(ks[3], (D_MODEL,)),
        "w_in": w(ks[4], (D_MODEL, IN_COLS), D_MODEL),
        "w_pool": w(ks[5], (N_POOL_GROUPS, POOL_GROUP_DIM, POOL_GROUP_DIM), POOL_GROUP_DIM),
        "pool_scale": 1.0 + 0.1 * jax.random.normal(ks[6], (N_POOL_GROUPS, POOL_GROUP_DIM), f32),
        "g_q_lora": gain(ks[7], (Q_LORA,)),
        "w_uq": w(ks[8], (Q_LORA, MLA_HEADS * QK_HEAD), Q_LORA),
        "g_kv_lora": gain(ks[9], (KV_LORA,)),
        "w_ukv": w(ks[10], (KV_LORA, MLA_HEADS * (QK_NOPE + V_HEAD)), KV_LORA),
        "g_q_mla": gain(ks[11], (QK_HEAD,)),
        "g_k_mla": gain(ks[12], (QK_HEAD,)),
        "g_mem_norm": gain(ks[13], (D_MODEL,)),
        "w_mem_kv": w(ks[14], (D_MODEL, 2 * CROSS_W), D_MODEL),
        "g_q_cross": gain(ks[15], (CROSS_HEAD_DIM,)),
        "g_k_cross": gain(ks[16], (CROSS_HEAD_DIM,)),
        "w_pool_out": w(ks[17], (POOL_W, D_MODEL), POOL_W),
        "w_mla_out": w(ks[18], (MLA_W, D_MODEL), MLA_W),
        "w_cross_out": w(ks[19], (CROSS_W, D_MODEL), CROSS_W),
        "w_o": w(ks[20], (D_MODEL, D_MODEL), D_MODEL),
        "g_ffn_norm": gain(ks[21], (D_MODEL,)),
        "w_router": w(ks[22], (D_MODEL, N_EXPERTS), D_MODEL),
        "b_router": 0.01 * jax.random.normal(ks[23], (N_EXPERTS,), f32),
        "w_exp_in": w(ks[24], (N_EXPERTS, D_MODEL, 2 * D_FF), D_MODEL),
        "b_exp_in": 0.01 * jax.random.normal(ks[25], (N_EXPERTS, 2 * D_FF), f32),
        "w_exp_out": w(ks[26], (N_EXPERTS, D_FF, D_MODEL), D_FF),
        "b_exp_out": 0.01 * jax.random.normal(ks[27], (N_EXPERTS, D_MODEL), f32),
    }


def reference(x, mem, positions, g_attn_norm, w_in, w_pool, pool_scale, g_q_lora, w_uq, g_kv_lora,
              w_ukv, g_q_mla, g_k_mla, g_mem_norm, w_mem_kv, g_q_cross, g_k_cross, w_pool_out,
              w_mla_out, w_cross_out, w_o, g_ffn_norm, w_router, b_router, w_exp_in, b_exp_in,
              w_exp_out, b_exp_out):
    for _ in range(DEPTH):
        h = rms_norm(x, g_attn_norm)
        z = h @ w_in
        u_pool = z[..., OFF_POOL:OFF_QL]
        c_q = z[..., OFF_QL:OFF_KVL]
        c_kv = z[..., OFF_KVL:OFF_KPE]
        k_pe = z[..., OFF_KPE:OFF_XQ]
        q_x = z[..., OFF_XQ:OFF_GATE]
        gates = jax.nn.sigmoid(z[..., OFF_GATE:].astype(jnp.float32)).astype(x.dtype)
        g_pool = gates[..., :D_MODEL]
        g_mla = gates[..., D_MODEL:2 * D_MODEL]
        g_cross = gates[..., 2 * D_MODEL:]

        y_pool = multiscale_pool(u_pool, w_pool, pool_scale) @ w_pool_out
        y_mla = mla_branch(c_q, c_kv, k_pe, positions, g_q_lora, w_uq, g_kv_lora, w_ukv,
                           g_q_mla, g_k_mla) @ w_mla_out
        y_cross = cross_branch(q_x, mem, g_mem_norm, w_mem_kv, g_q_cross, g_k_cross) @ w_cross_out

        merged = g_pool * y_pool + g_mla * y_mla + g_cross * y_cross
        x = x + merged @ w_o
        x = x + moe_ffn(rms_norm(x, g_ffn_norm), w_router, b_router, w_exp_in, b_exp_in,
                        w_exp_out, b_exp_out)
    return x
```

```python
import functools

import jax
import jax.numpy as jnp
from jax import lax
from jax.experimental import pallas as pl
from jax.experimental.pallas import tpu as pltpu

F32 = jnp.float32
BF16 = jnp.bfloat16
U32 = jnp.uint32
I32 = jnp.int32

EPS = 1e-6
ROPE_BASE = 10000.0
LANES = 128
SUBLANES = 8
POOL_WINDOWS = (2, 4, 8, 16)
POOL_TAIL = 16
POOL_W = 512
MLA_HEADS = 8
QK_NOPE = 64
QK_ROPE = 32
QK_HEAD = QK_NOPE + QK_ROPE
V_HEAD = 64
Q_LORA = 384
KV_LORA = 256
CROSS_HEADS = 4
CROSS_HEAD_DIM = 128
CROSS_W = 512
N_EXPERTS = 32
TOP_K = 4
SWIGLU_LIMIT = 7.0
SWIGLU_ALPHA = 1.702
ROW_BLOCK = 512
LOG2E = 1.4426950408889634
NEG_BIG = -1e30

C_POOL = 0
C_QL = C_POOL + POOL_W
C_KVL = C_QL + Q_LORA
C_KPE = C_KVL + KV_LORA
C_XQ = C_KPE + LANES
C_GATE = C_XQ + CROSS_W

VMEM_LIMIT = 56 * 1024 * 1024


def _dot(a, b):
    return jnp.dot(a, b, preferred_element_type=F32)


def _dot_nt(a, b):
    return lax.dot_general(a, b, (((1,), (1,)), ((), ())), preferred_element_type=F32)


def _rms(xf, g, n=None):
    n = xf.shape[-1] if n is None else n
    ms = jnp.sum(xf * xf, axis=-1, keepdims=True) * (1.0 / n)
    return xf * lax.rsqrt(ms + EPS) * g


def _pack_bf16_pair(lo, hi):
    lo_b = lax.bitcast_convert_type(lo.astype(BF16).astype(F32), U32)
    hi_b = lax.bitcast_convert_type(hi.astype(BF16).astype(F32), U32)
    return (lo_b >> 16) | (hi_b & jnp.uint32(0xFFFF0000))


def _unpack_bf16_pair(w):
    lo = lax.bitcast_convert_type(w << 16, F32)
    hi = lax.bitcast_convert_type(w & jnp.uint32(0xFFFF0000), F32)
    return lo, hi


def _memkv_kernel(mem_ref, g_ref, w_ref, gk_ref, k_ref, v_ref):
    h = _rms(mem_ref[0], g_ref[...]).astype(BF16)
    kv = _dot(h, w_ref[...])
    for c in range(CROSS_HEADS):
        sl = slice(c * CROSS_HEAD_DIM, (c + 1) * CROSS_HEAD_DIM)
        k_ref[0, :, sl] = _rms(kv[:, sl], gk_ref[...]).astype(BF16)
    v_ref[0] = kv[:, CROSS_W:].astype(BF16)


def _stage1_kernel(x_ref, pos_ref, kc_ref, vc_ref, win_ref, gattn_ref, wpool_ref, pscale_ref,
                   gql_ref, wuq_ref, gkvl_ref, wuk_ref, wuv_ref, gq_ref, gk_ref, invf_ref, gqx_ref,
                   ypool_ref, q_ref, k_ref, v_ref, oc_ref, gates_ref, tail_ref, *, tm):
    i = pl.program_id(1)

    @pl.when(i == 0)
    def _():
        tail_ref[...] = jnp.zeros_like(tail_ref)

    h = _rms(x_ref[0], gattn_ref[...]).astype(BF16)

    u = _dot(h, win_ref[:, C_POOL:C_QL])
    ext = jnp.concatenate([tail_ref[...], u], axis=0)
    tail_ref[...] = u[tm - POOL_TAIL:, :]
    tpos = (i * tm + lax.broadcasted_iota(I32, (tm, 1), 0) + 1).astype(F32)
    run = ext
    shift = 1
    for g, win in enumerate(POOL_WINDOWS):
        while shift < win:
            run = run + pltpu.roll(run, shift, 0)
            shift *= 2
        sl = slice(g * LANES, (g + 1) * LANES)
        cnt = jnp.minimum(tpos, float(win))
        mixed = run[POOL_TAIL:, sl] / cnt - u[:, sl]
        y = _dot(mixed.astype(BF16), wpool_ref[g]) * pscale_ref[:, sl]
        ypool_ref[0, :, sl] = y.astype(BF16)

    lane = lax.broadcasted_iota(I32, (1, LANES), 1)
    ang = pos_ref[0].astype(F32) * invf_ref[...]
    cosf = jnp.cos(ang)
    sinf = jnp.sin(ang)
    half = QK_ROPE // 2
    sin_a = jnp.where((lane >= QK_NOPE) & (lane < QK_NOPE + half), -sinf, 0.0)
    sin_b = jnp.where((lane >= QK_NOPE + half) & (lane < QK_HEAD), sinf, 0.0)

    def rope(t):
        return t * cosf + pltpu.roll(t, LANES - half, 1) * sin_a + pltpu.roll(t, half, 1) * sin_b

    cq = _dot(h, win_ref[:, C_QL:C_KVL])
    qf = _dot(_rms(cq, gql_ref[...]).astype(BF16), wuq_ref[...])
    qscale = QK_HEAD ** -0.5 * LOG2E
    for hh in range(MLA_HEADS):
        sl = slice(hh * LANES, (hh + 1) * LANES)
        qn = _rms(qf[:, sl], gq_ref[...], QK_HEAD)
        q_ref[0, :, sl] = (rope(qn) * qscale).astype(BF16)

    ckv = _dot(h, win_ref[:, C_KVL:C_KPE])
    ckvn = _rms(ckv, gkvl_ref[...]).astype(BF16)
    kf = _dot(ckvn, wuk_ref[...])
    kpe = pltpu.roll(_dot(h, win_ref[:, C_KPE:C_XQ]), QK_NOPE, 1)
    for hh in range(MLA_HEADS):
        sl = slice(hh * LANES, (hh + 1) * LANES)
        kn = _rms(kf[:, sl] + kpe, gk_ref[...], QK_HEAD)
        k_ref[0, :, sl] = rope(kn).astype(BF16)
    v_ref[0] = _dot(ckvn, wuv_ref[...]).astype(BF16)

    qx = _dot(h, win_ref[:, C_XQ:C_GATE])
    xscale = CROSS_HEAD_DIM ** -0.5 * LOG2E
    for c in range(CROSS_HEADS):
        sl = slice(c * CROSS_HEAD_DIM, (c + 1) * CROSS_HEAD_DIM)
        qc = (_rms(qx[:, sl], gqx_ref[...]) * xscale).astype(BF16)
        s = _dot_nt(qc, kc_ref[0, :, sl])
        p = jnp.exp2(s - jnp.max(s, axis=-1, keepdims=True))
        o = _dot(p.astype(BF16), vc_ref[0, :, sl]) / jnp.sum(p, axis=-1, keepdims=True)
        oc_ref[0, :, sl] = o.astype(BF16)

    d = x_ref.shape[-1]
    for g in range(3):
        zg = _dot(h, win_ref[:, C_GATE + g * d:C_GATE + (g + 1) * d])
        gates_ref[0, :, g * d:(g + 1) * d] = jax.nn.sigmoid(zg).astype(BF16)


def _mla_kernel(q_ref, k_ref, v_ref, o_ref, *, tq):
    i = pl.program_id(2)
    lane = lax.broadcasted_iota(I32, (1, LANES), 1)
    row = lax.broadcasted_iota(I32, (tq, tq), 0)
    col = lax.broadcasted_iota(I32, (tq, tq), 1)
    outs = []
    for hh in range(2):
        q = q_ref[0, :, hh * LANES:(hh + 1) * LANES]

        def step(j, carry, masked):
            m, l, acc = carry
            off = pl.multiple_of(j * tq, tq)
            kt = k_ref[0, pl.ds(off, tq), hh * LANES:(hh + 1) * LANES]
            vt = v_ref[0, pl.ds(off, tq), :]
            s = _dot_nt(q, kt)
            if masked:
                s = jnp.where(col <= row, s, NEG_BIG)
            m_new = jnp.maximum(m, jnp.max(s, axis=-1, keepdims=True))
            a = jnp.exp2(m - m_new)
            p = jnp.exp2(s - m_new)
            l = a * l + jnp.sum(p, axis=-1, keepdims=True)
            acc = a * acc + _dot(p.astype(BF16), vt)
            return m_new, l, acc

        init = (jnp.full((tq, 1), NEG_BIG, F32), jnp.zeros((tq, 1), F32), jnp.zeros((tq, LANES), F32))
        carry = lax.fori_loop(0, i, functools.partial(step, masked=False), init)
        m, l, acc = step(i, carry, True)
        outs.append(acc / l)
    o_ref[0] = jnp.where(lane < V_HEAD, outs[0], outs[1]).astype(BF16)


def _stage3_kernel(yp_ref, om_ref, oc_ref, gates_ref, x_ref, wpo_ref, wmo_ref, wco_ref, wo_ref,
                   gffn_ref, wr_ref, br_ref,
                   x2_ref, h2p_ref, idx_ref, gate_ref, rank_ref, cnt_ref, carry_ref, *, tm):
    i = pl.program_id(0)

    @pl.when(i == 0)
    def _():
        carry_ref[...] = jnp.zeros_like(carry_ref)

    d = x_ref.shape[-1]
    merged = (gates_ref[:, 0:d].astype(F32) * _dot(yp_ref[...], wpo_ref[...])
              + gates_ref[:, d:2 * d].astype(F32) * _dot(om_ref[...], wmo_ref[...])
              + gates_ref[:, 2 * d:3 * d].astype(F32) * _dot(oc_ref[...], wco_ref[...]))
    x2 = x_ref[...] + _dot(merged.astype(BF16), wo_ref[...])
    x2_ref[...] = x2
    h2 = _rms(x2, gffn_ref[...])
    h2p_ref[...] = _pack_bf16_pair(h2[:, :d // 2], h2[:, d // 2:])

    logits = _dot(h2.astype(BF16), wr_ref[...]) + br_ref[...]
    lanef = lax.broadcasted_iota(I32, (tm, LANES), 1).astype(F32)
    rem = logits
    tops, hots = [], []
    for _ in range(TOP_K):
        m = jnp.max(rem, axis=-1, keepdims=True)
        ix = jnp.min(jnp.where(rem == m, lanef, float(LANES)), axis=-1, keepdims=True)
        hot = lanef == ix
        tops.append((m, ix))
        hots.append(hot)
        rem = jnp.where(hot, NEG_BIG * 2, rem)
    es = [jnp.exp(m - tops[0][0]) for m, _ in tops]
    denom = es[0] + es[1] + es[2] + es[3]

    onehot = jnp.zeros((tm, LANES), F32)
    for hot in hots:
        onehot = onehot + hot.astype(F32)
    r = lax.broadcasted_iota(I32, (tm, tm), 0)
    c = lax.broadcasted_iota(I32, (tm, tm), 1)
    tri = (c < r).astype(BF16)
    before = _dot(tri, onehot.astype(BF16)) + carry_ref[...]
    carry_new = carry_ref[...] + jnp.sum(onehot, axis=0, keepdims=True)
    carry_ref[...] = carry_new
    cnt_ref[...] = carry_new

    idx_out = jnp.zeros((tm, LANES), F32)
    gate_out = jnp.zeros((tm, LANES), F32)
    rank_out = jnp.zeros((tm, LANES), F32)
    for k in range(TOP_K):
        rk = jnp.sum(jnp.where(hots[k], before, 0.0), axis=-1, keepdims=True)
        idx_out = jnp.where(lanef == float(k), tops[k][1], idx_out)
        gate_out = jnp.where(lanef == float(k), es[k] / denom, gate_out)
        rank_out = jnp.where(lanef == float(k), rk, rank_out)
    idx_ref[...] = idx_out.astype(I32)
    gate_ref[...] = gate_out
    rank_ref[...] = rank_out.astype(I32)


def _moe_kernel(be_ref, nvalid_ref, tok_ref, tgt_ref, h2p_hbm, w1_ref, b1_ref, w2_ref, b2_ref,
                out_hbm, xbuf, obuf, gsem, ssem):
    b = pl.program_id(0)
    half = xbuf.shape[1]
    ff = w2_ref.shape[1]
    nv = nvalid_ref[b]

    @pl.when(b == 0)
    def _():
        xbuf[...] = jnp.zeros_like(xbuf)

    def rows_copy(src, s, dst, d, n, sem):
        return pltpu.make_async_copy(src.at[pl.ds(s, n), :], dst.at[pl.ds(d, n), :], sem)

    def wait_rows(src, dst, sem):
        bulk = pl.multiple_of((nv // SUBLANES) * SUBLANES, SUBLANES)

        @pl.when(bulk > 0)
        def _():
            rows_copy(src, 0, dst, 0, bulk, sem).wait()

        def one(r, c):
            rows_copy(src, 0, dst, 0, 1, sem).wait()
            return c

        lax.fori_loop(bulk, nv, one, 0)

    @pl.when(nv > 0)
    def _():
        def gather(r, c):
            rows_copy(h2p_hbm, tok_ref[0, 0, r], xbuf, r, 1, gsem).start()
            return c

        lax.fori_loop(0, nv, gather, 0)
        wait_rows(h2p_hbm, xbuf, gsem)

        lo, hi = _unpack_bf16_pair(xbuf[...])
        hm = (_dot(lo.astype(BF16), w1_ref[0, :half, :]) + _dot(hi.astype(BF16), w1_ref[0, half:, :])
              + b1_ref[0])
        glu = jnp.minimum(hm[:, :ff], SWIGLU_LIMIT)
        lin = jnp.clip(hm[:, ff:], -SWIGLU_LIMIT, SWIGLU_LIMIT)
        act = glu * jax.nn.sigmoid(SWIGLU_ALPHA * glu) * (lin + 1.0)
        o = _dot(act.astype(BF16), w2_ref[0]) + b2_ref[0]
        obuf[...] = _pack_bf16_pair(o[:, :half], o[:, half:])

        def scatter(r, c):
            rows_copy(obuf, r, out_hbm, tgt_ref[0, 0, r], 1, ssem).start()
            return c

        lax.fori_loop(0, nv, scatter, 0)
        wait_rows(obuf, out_hbm, ssem)


def _combine_kernel(x2_ref, o0_ref, o1_ref, o2_ref, o3_ref, gate_ref, out_ref):
    half = o0_ref.shape[1]
    acc_lo = x2_ref[:, :half]
    acc_hi = x2_ref[:, half:]
    for k, o_ref in enumerate((o0_ref, o1_ref, o2_ref, o3_ref)):
        lo, hi = _unpack_bf16_pair(o_ref[...])
        g = gate_ref[:, k:k + 1]
        acc_lo = acc_lo + g * lo
        acc_hi = acc_hi + g * hi
    out_ref[:, :half] = acc_lo
    out_ref[:, half:] = acc_hi


def _full(shape):
    return pl.BlockSpec(shape, lambda *_: (0,) * len(shape))


def _params(*semantics):
    return pltpu.CompilerParams(dimension_semantics=semantics, vmem_limit_bytes=VMEM_LIMIT)


def _memkv_call(mem, g_mem, w_mem_kv, g_k_cross):
    B, M, D = mem.shape
    return pl.pallas_call(
        _memkv_kernel,
        grid=(B,),
        in_specs=[pl.BlockSpec((1, M, D), lambda b: (b, 0, 0)), _full((1, D)), _full((D, 2 * CROSS_W)),
                  _full((1, CROSS_HEAD_DIM))],
        out_specs=[pl.BlockSpec((1, M, CROSS_W), lambda b: (b, 0, 0))] * 2,
        out_shape=[jax.ShapeDtypeStruct((B, M, CROSS_W), BF16)] * 2,
        compiler_params=_params("arbitrary"),
        name="memkv",
    )(mem, g_mem, w_mem_kv, g_k_cross)


def _stage1_call(x, pos, kc, vc, win, g_attn, wpool, pscale, gql, wuq, gkvl, wuk, wuv, gq, gk, invf, gqx, *, tm):
    B, S, D = x.shape
    M = kc.shape[1]
    tile = lambda w: pl.BlockSpec((1, tm, w), lambda b, i: (b, i, 0))
    widths = (POOL_W, MLA_HEADS * LANES, MLA_HEADS * LANES, MLA_HEADS * V_HEAD, CROSS_W, 3 * D)
    return pl.pallas_call(
        functools.partial(_stage1_kernel, tm=tm),
        grid=(B, S // tm),
        in_specs=[tile(D), tile(1),
                  pl.BlockSpec((1, M, CROSS_W), lambda b, i: (b, 0, 0)),
                  pl.BlockSpec((1, M, CROSS_W), lambda b, i: (b, 0, 0)),
                  _full(win.shape), _full((1, D)), _full(wpool.shape), _full((1, POOL_W)),
                  _full((1, Q_LORA)), _full(wuq.shape), _full((1, KV_LORA)), _full(wuk.shape), _full(wuv.shape),
                  _full((1, LANES)), _full((1, LANES)), _full((1, LANES)), _full((1, CROSS_HEAD_DIM))],
        out_specs=[tile(w) for w in widths],
        out_shape=[jax.ShapeDtypeStruct((B, S, w), BF16) for w in widths],
        scratch_shapes=[pltpu.VMEM((POOL_TAIL, POOL_W), F32)],
        compiler_params=_params("arbitrary", "arbitrary"),
        name="stage1",
    )(x, pos, kc, vc, win, g_attn, wpool, pscale, gql, wuq, gkvl, wuk, wuv, gq, gk, invf, gqx)


def _mla_call(q, k, v, *, tq):
    B, S, _ = q.shape
    return pl.pallas_call(
        functools.partial(_mla_kernel, tq=tq),
        grid=(B, MLA_HEADS // 2, S // tq),
        in_specs=[pl.BlockSpec((1, tq, 2 * LANES), lambda b, hp, i: (b, i, hp)),
                  pl.BlockSpec((1, S, 2 * LANES), lambda b, hp, i: (b, 0, hp)),
                  pl.BlockSpec((1, S, 2 * V_HEAD), lambda b, hp, i: (b, 0, hp))],
        out_specs=pl.BlockSpec((1, tq, 2 * V_HEAD), lambda b, hp, i: (b, i, hp)),
        out_shape=jax.ShapeDtypeStruct((B, S, MLA_HEADS * V_HEAD), BF16),
        compiler_params=_params("arbitrary", "arbitrary", "arbitrary"),
        name="mla",
    )(q, k, v)


def _stage3_call(ypool, omla, ocross, gates, x, wpo, wmo, wco, wo, g_ffn, wr, br, *, tm):
    T, D = x.shape
    trow = lambda w: pl.BlockSpec((tm, w), lambda i: (i, 0))
    return pl.pallas_call(
        functools.partial(_stage3_kernel, tm=tm),
        grid=(T // tm,),
        in_specs=[trow(POOL_W), trow(MLA_HEADS * V_HEAD), trow(CROSS_W), trow(3 * D), trow(D),
                  _full((POOL_W, D)), _full((MLA_HEADS * V_HEAD, D)), _full((CROSS_W, D)), _full((D, D)),
                  _full((1, D)), _full((D, LANES)), _full((1, LANES))],
        out_specs=[trow(D), trow(D // 2), trow(LANES), trow(LANES), trow(LANES), _full((1, LANES))],
        out_shape=[jax.ShapeDtypeStruct((T, D), F32), jax.ShapeDtypeStruct((T, D // 2), U32),
                   jax.ShapeDtypeStruct((T, LANES), I32), jax.ShapeDtypeStruct((T, LANES), F32),
                   jax.ShapeDtypeStruct((T, LANES), I32), jax.ShapeDtypeStruct((1, LANES), F32)],
        scratch_shapes=[pltpu.VMEM((1, LANES), F32)],
        compiler_params=_params("arbitrary"),
        name="stage3",
    )(ypool, omla, ocross, gates, x, wpo, wmo, wco, wo, g_ffn, wr, br)


def _moe_call(block_e, nvalid, tok, tgt, h2p, w1, b1, w2, b2):
    NB, _, R = tok.shape
    T, halfd = h2p.shape
    _, D, FF2 = w1.shape
    smem_rows = pl.BlockSpec((1, 1, R), lambda b, be, nv: (b, 0, 0), memory_space=pltpu.SMEM)
    return pl.pallas_call(
        _moe_kernel,
        grid_spec=pltpu.PrefetchScalarGridSpec(
            num_scalar_prefetch=2,
            grid=(NB,),
            in_specs=[smem_rows, smem_rows,
                      pl.BlockSpec(memory_space=pl.ANY),
                      pl.BlockSpec((1, D, FF2), lambda b, be, nv: (be[b], 0, 0)),
                      pl.BlockSpec((1, 1, FF2), lambda b, be, nv: (be[b], 0, 0)),
                      pl.BlockSpec((1, FF2 // 2, D), lambda b, be, nv: (be[b], 0, 0)),
                      pl.BlockSpec((1, 1, D), lambda b, be, nv: (be[b], 0, 0))],
            out_specs=pl.BlockSpec(memory_space=pl.ANY),
            scratch_shapes=[pltpu.VMEM((R, halfd), U32), pltpu.VMEM((R, halfd), U32),
                            pltpu.SemaphoreType.DMA, pltpu.SemaphoreType.DMA]),
        out_shape=jax.ShapeDtypeStruct((TOP_K * T, halfd), U32),
        compiler_params=_params("arbitrary"),
        name="moe",
    )(block_e, nvalid, tok, tgt, h2p, w1, b1, w2, b2)


def _combine_call(x2, out4, gate, *, tm):
    T, D = x2.shape
    nt = T // tm
    trow = lambda w: pl.BlockSpec((tm, w), lambda i: (i, 0))
    return pl.pallas_call(
        _combine_kernel,
        grid=(nt,),
        in_specs=[trow(D)] + [pl.BlockSpec((tm, D // 2), lambda i, kk=kk: (kk * nt + i, 0)) for kk in range(TOP_K)]
                 + [trow(LANES)],
        out_specs=trow(D),
        out_shape=jax.ShapeDtypeStruct((T, D), F32),
        compiler_params=_params("arbitrary"),
        name="combine",
    )(x2, out4, out4, out4, out4, gate)


def kernel(x, mem, positions, g_attn_norm, w_in, w_pool, pool_scale, g_q_lora, w_uq, g_kv_lora, w_ukv, g_q_mla, g_k_mla, g_mem_norm, w_mem_kv, g_q_cross, g_k_cross, w_pool_out, w_mla_out, w_cross_out, w_o, g_ffn_norm, w_router, b_router, w_exp_in, b_exp_in, w_exp_out, b_exp_out):
    B, S, D = x.shape
    T = B * S
    N = T * TOP_K
    E = N_EXPERTS
    R = ROW_BLOCK
    NB = (N + E * (R - 1) + R - 1) // R
    tm = min(512, S)
    tq = min(512, S)
    assert S % tm == 0 and T % tm == 0 and tm >= POOL_TAIL

    zpad = jnp.zeros((D, LANES - QK_ROPE), w_in.dtype)
    win = jnp.concatenate([w_in[:, :C_KPE + QK_ROPE], zpad, w_in[:, C_KPE + QK_ROPE:]], axis=1).astype(BF16)
    wuq = jnp.pad(w_uq.reshape(Q_LORA, MLA_HEADS, QK_HEAD), ((0, 0), (0, 0), (0, LANES - QK_HEAD)))
    wuq = wuq.reshape(Q_LORA, MLA_HEADS * LANES).astype(BF16)
    wukv = w_ukv.reshape(KV_LORA, MLA_HEADS, QK_NOPE + V_HEAD)
    wuk = jnp.pad(wukv[:, :, :QK_NOPE], ((0, 0), (0, 0), (0, LANES - QK_NOPE)))
    wuk = wuk.reshape(KV_LORA, MLA_HEADS * LANES).astype(BF16)
    wuv = wukv[:, :, QK_NOPE:].reshape(KV_LORA, MLA_HEADS * V_HEAD).astype(BF16)
    gq = jnp.pad(g_q_mla, (0, LANES - QK_HEAD)).reshape(1, LANES)
    gk = jnp.pad(g_k_mla, (0, LANES - QK_HEAD)).reshape(1, LANES)
    half = QK_ROPE // 2
    inv = ROPE_BASE ** (-jnp.arange(half, dtype=F32) / half)
    invf = jnp.zeros((LANES,), F32).at[QK_NOPE:QK_NOPE + half].set(inv).at[QK_NOPE + half:QK_HEAD].set(inv)
    invf = invf.reshape(1, LANES)
    wr = jnp.pad(w_router, ((0, 0), (0, LANES - E))).astype(BF16)
    br = jnp.concatenate([b_router, jnp.full((LANES - E,), NEG_BIG, F32)]).reshape(1, LANES)
    row = lambda v: v.reshape(1, -1)
    flat = lambda a: a.reshape(T, a.shape[-1])

    kc, vc = _memkv_call(mem, row(g_mem_norm), w_mem_kv.astype(BF16), row(g_k_cross))
    ypool, q, k, v, ocross, gates = _stage1_call(
        x, positions.reshape(B, S, 1), kc, vc, win, row(g_attn_norm), w_pool.astype(BF16),
        pool_scale.reshape(1, POOL_W), row(g_q_lora), wuq, row(g_kv_lora), wuk, wuv, gq, gk, invf,
        row(g_q_cross), tm=tm)
    omla = _mla_call(q, k, v, tq=tq)
    x2, h2p, idx, gate, rank, cnt = _stage3_call(
        flat(ypool), flat(omla), flat(ocross), flat(gates), flat(x), w_pool_out.astype(BF16),
        w_mla_out.astype(BF16), w_cross_out.astype(BF16), w_o.astype(BF16), row(g_ffn_norm), wr, br, tm=tm)

    counts = cnt[0, :E].astype(I32)
    padded = (counts + R - 1) // R * R
    pend = jnp.cumsum(padded)
    pstart = pend - padded
    dest = (pstart[idx[:, :TOP_K]] + rank[:, :TOP_K]).reshape(-1)
    row0 = jnp.arange(NB, dtype=I32) * R
    block_e = jnp.minimum(jnp.searchsorted(pend, row0, side='right'), E - 1).astype(I32)
    nvalid = jnp.clip(pstart[block_e] + counts[block_e] - row0, 0, R).astype(I32)
    src = jnp.zeros((NB * R,), I32).at[dest].set(jnp.arange(N, dtype=I32))
    tok = (src // TOP_K).reshape(NB, 1, R)
    tgt = ((src % TOP_K) * T + src // TOP_K).reshape(NB, 1, R)

    FF2 = w_exp_in.shape[-1]
    out4 = _moe_call(block_e, nvalid, tok, tgt, h2p, w_exp_in.astype(BF16), b_exp_in.reshape(E, 1, FF2),
                     w_exp_out.astype(BF16), b_exp_out.reshape(E, 1, D))
    return _combine_call(x2, out4, gate, tm=tm).reshape(B, S, D)
```

```python
import functools

import jax
import jax.numpy as jnp
from jax import lax
from jax.experimental import pallas as pl
from jax.experimental.pallas import tpu as pltpu

F32 = jnp.float32
BF16 = jnp.bfloat16
U32 = jnp.uint32
I32 = jnp.int32

EPS = 1e-6
ROPE_BASE = 10000.0
LANES = 128
SUBLANES = 8
MXU_COLS = 256
POOL_WINDOWS = (2, 4, 8, 16)
POOL_TAIL = 16
POOL_W = 512
MLA_HEADS = 8
QK_NOPE = 64
QK_ROPE = 32
QK_HEAD = QK_NOPE + QK_ROPE
V_HEAD = 64
Q_LORA = 384
KV_LORA = 256
CROSS_HEADS = 4
CROSS_HEAD_DIM = 128
CROSS_W = 512
N_EXPERTS = 32
TOP_K = 4
SWIGLU_LIMIT = 7.0
SWIGLU_ALPHA = 1.702
ROW_BLOCK = 512
LOG2E = 1.4426950408889634
NEG_BIG = -1e30

C_POOL = 0
C_QL = C_POOL + POOL_W
C_KVL = C_QL + Q_LORA
C_KPE = C_KVL + KV_LORA
C_XQ = C_KPE + LANES
C_GATE = C_XQ + CROSS_W

VMEM_LIMIT = 56 * 1024 * 1024


def _dot(a, b):
    return jnp.dot(a, b, preferred_element_type=F32)


def _dot_nt(a, b):
    return lax.dot_general(a, b, (((1,), (1,)), ((), ())), preferred_element_type=F32)


def _rms(xf, g, n=None):
    n = xf.shape[-1] if n is None else n
    ms = jnp.sum(xf * xf, axis=-1, keepdims=True) * (1.0 / n)
    return xf * lax.rsqrt(ms + EPS) * g


def _pack_bf16_pair(lo, hi):
    lo_b = lax.bitcast_convert_type(lo.astype(BF16).astype(F32), U32)
    hi_b = lax.bitcast_convert_type(hi.astype(BF16).astype(F32), U32)
    return (lo_b >> 16) | (hi_b & jnp.uint32(0xFFFF0000))


def _unpack_bf16_pair(w):
    lo = lax.bitcast_convert_type(w << 16, F32)
    hi = lax.bitcast_convert_type(w & jnp.uint32(0xFFFF0000), F32)
    return lo, hi


def _memkv_kernel(mem_ref, g_ref, w_ref, gk_ref, k_ref, v_ref):
    h = _rms(mem_ref[0], g_ref[...]).astype(BF16)
    kv = _dot(h, w_ref[...])
    for c in range(CROSS_HEADS):
        sl = slice(c * CROSS_HEAD_DIM, (c + 1) * CROSS_HEAD_DIM)
        k_ref[0, :, sl] = _rms(kv[:, sl], gk_ref[...]).astype(BF16)
    v_ref[0] = kv[:, CROSS_W:].astype(BF16)


def _stage1_kernel(x_ref, pos_ref, kc_ref, vc_ref, win_ref, gattn_ref, wpool_ref, pscale_ref,
                   gql_ref, wuq_ref, gkvl_ref, wuk_ref, wuv_ref, gq_ref, gk_ref, invf_ref, gqx_ref,
                   ypool_ref, q_ref, k_ref, v_ref, oc_ref, gates_ref, tail_ref, *, tm):
    i = pl.program_id(1)

    @pl.when(i == 0)
    def _():
        tail_ref[...] = jnp.zeros_like(tail_ref)

    h = _rms(x_ref[0], gattn_ref[...]).astype(BF16)

    u = _dot(h, win_ref[:, C_POOL:C_QL])
    ext = jnp.concatenate([tail_ref[...], u], axis=0)
    tail_ref[...] = u[tm - POOL_TAIL:, :]
    tpos = (i * tm + lax.broadcasted_iota(I32, (tm, 1), 0) + 1).astype(F32)
    run = ext
    shift = 1
    for g, win in enumerate(POOL_WINDOWS):
        while shift < win:
            run = run + pltpu.roll(run, shift, 0)
            shift *= 2
        sl = slice(g * LANES, (g + 1) * LANES)
        cnt = jnp.minimum(tpos, float(win))
        mixed = run[POOL_TAIL:, sl] / cnt - u[:, sl]
        y = _dot(mixed.astype(BF16), wpool_ref[g]) * pscale_ref[:, sl]
        ypool_ref[0, :, sl] = y.astype(BF16)

    lane = lax.broadcasted_iota(I32, (1, LANES), 1)
    ang = pos_ref[0].astype(F32) * invf_ref[...]
    cosf = jnp.cos(ang)
    sinf = jnp.sin(ang)
    half = QK_ROPE // 2
    sin_a = jnp.where((lane >= QK_NOPE) & (lane < QK_NOPE + half), -sinf, 0.0)
    sin_b = jnp.where((lane >= QK_NOPE + half) & (lane < QK_HEAD), sinf, 0.0)

    def rope(t):
        return t * cosf + pltpu.roll(t, LANES - half, 1) * sin_a + pltpu.roll(t, half, 1) * sin_b

    cq = _dot(h, win_ref[:, C_QL:C_KVL])
    qf = _dot(_rms(cq, gql_ref[...]).astype(BF16), wuq_ref[...])
    qscale = QK_HEAD ** -0.5 * LOG2E
    for hh in range(MLA_HEADS):
        sl = slice(hh * LANES, (hh + 1) * LANES)
        qn = _rms(qf[:, sl], gq_ref[...], QK_HEAD)
        q_ref[0, :, sl] = (rope(qn) * qscale).astype(BF16)

    ckv = _dot(h, win_ref[:, C_KVL:C_KPE])
    ckvn = _rms(ckv, gkvl_ref[...]).astype(BF16)
    kf = _dot(ckvn, wuk_ref[...])
    kpe = pltpu.roll(_dot(h, win_ref[:, C_KPE:C_XQ]), QK_NOPE, 1)
    for hh in range(MLA_HEADS):
        sl = slice(hh * LANES, (hh + 1) * LANES)
        kn = _rms(kf[:, sl] + kpe, gk_ref[...], QK_HEAD)
        k_ref[0, :, sl] = rope(kn).astype(BF16)
    v_ref[0] = _dot(ckvn, wuv_ref[...]).astype(BF16)

    qx = _dot(h, win_ref[:, C_XQ:C_GATE])
    xscale = CROSS_HEAD_DIM ** -0.5 * LOG2E
    for c in range(CROSS_HEADS):
        sl = slice(c * CROSS_HEAD_DIM, (c + 1) * CROSS_HEAD_DIM)
        qc = (_rms(qx[:, sl], gqx_ref[...]) * xscale).astype(BF16)
        s = _dot_nt(qc, kc_ref[0, :, sl])
        p = jnp.exp2(s - jnp.max(s, axis=-1, keepdims=True))
        o = _dot(p.astype(BF16), vc_ref[0, :, sl]) / jnp.sum(p, axis=-1, keepdims=True)
        oc_ref[0, :, sl] = o.astype(BF16)

    d = x_ref.shape[-1]
    for g in range(3):
        zg = _dot(h, win_ref[:, C_GATE + g * d:C_GATE + (g + 1) * d])
        gates_ref[0, :, g * d:(g + 1) * d] = jax.nn.sigmoid(zg).astype(BF16)


def _mla_kernel(q_ref, k_ref, v_ref, o_ref, *, tq):
    i = pl.program_id(2)
    lane = lax.broadcasted_iota(I32, (1, LANES), 1)
    row = lax.broadcasted_iota(I32, (tq, tq), 0)
    col = lax.broadcasted_iota(I32, (tq, tq), 1)
    qs = [q_ref[0, :, hh * LANES:(hh + 1) * LANES] for hh in range(2)]

    def step(j, carry, masked):
        off = pl.multiple_of(j * tq, tq)
        vt = v_ref[0, pl.ds(off, tq), :]
        new = []
        for hh in range(2):
            m, l, acc = carry[hh]
            kt = k_ref[0, pl.ds(off, tq), hh * LANES:(hh + 1) * LANES]
            s = _dot_nt(qs[hh], kt)
            if masked:
                s = jnp.where(col <= row, s, NEG_BIG)
            m_new = jnp.maximum(m, jnp.max(s, axis=-1, keepdims=True))
            a = jnp.exp2(m - m_new)
            p = jnp.exp2(s - m_new)
            l = a * l + jnp.sum(p, axis=-1, keepdims=True)
            acc = a * acc + _dot(p.astype(BF16), vt)
            new.append((m_new, l, acc))
        return tuple(new)

    init = (jnp.full((tq, 1), NEG_BIG, F32), jnp.zeros((tq, 1), F32), jnp.zeros((tq, LANES), F32))
    carry = lax.fori_loop(0, i, functools.partial(step, masked=False), (init, init))
    (_, l0, acc0), (_, l1, acc1) = step(i, carry, True)
    o_ref[0] = jnp.where(lane < V_HEAD, acc0 / l0, acc1 / l1).astype(BF16)


def _stage3_kernel(yp_ref, om_ref, oc_ref, gates_ref, x_ref, wpo_ref, wmo_ref, wco_ref, wo_ref,
                   gffn_ref, wr_ref, br_ref,
                   x2_ref, h2p_ref, idx_ref, gate_ref, rank_ref, cnt_ref, carry_ref, *, tm):
    i = pl.program_id(0)

    @pl.when(i == 0)
    def _():
        carry_ref[...] = jnp.zeros_like(carry_ref)

    d = x_ref.shape[-1]
    merged = (gates_ref[:, 0:d].astype(F32) * _dot(yp_ref[...], wpo_ref[...])
              + gates_ref[:, d:2 * d].astype(F32) * _dot(om_ref[...], wmo_ref[...])
              + gates_ref[:, 2 * d:3 * d].astype(F32) * _dot(oc_ref[...], wco_ref[...]))
    x2 = x_ref[...] + _dot(merged.astype(BF16), wo_ref[...])
    x2_ref[...] = x2
    h2 = _rms(x2, gffn_ref[...])
    h2p = _pack_bf16_pair(h2[:, :d // 2], h2[:, d // 2:])
    for j in range(h2p_ref.shape[1]):
        h2p_ref[:, j, :] = h2p[:, j * LANES:(j + 1) * LANES]

    logits = _dot(h2.astype(BF16), wr_ref[...]) + br_ref[...]
    lanef = lax.broadcasted_iota(I32, (tm, LANES), 1).astype(F32)
    rem = logits
    tops, hots = [], []
    for _ in range(TOP_K):
        m = jnp.max(rem, axis=-1, keepdims=True)
        ix = jnp.min(jnp.where(rem == m, lanef, float(LANES)), axis=-1, keepdims=True)
        hot = lanef == ix
        tops.append((m, ix))
        hots.append(hot)
        rem = jnp.where(hot, NEG_BIG * 2, rem)
    es = [jnp.exp(m - tops[0][0]) for m, _ in tops]
    denom = es[0] + es[1] + es[2] + es[3]

    onehot = jnp.zeros((tm, LANES), F32)
    for hot in hots:
        onehot = onehot + hot.astype(F32)
    r = lax.broadcasted_iota(I32, (tm, tm), 0)
    c = lax.broadcasted_iota(I32, (tm, tm), 1)
    tri = (c < r).astype(BF16)
    before = _dot(tri, onehot.astype(BF16)) + carry_ref[...]
    carry_new = carry_ref[...] + jnp.sum(onehot, axis=0, keepdims=True)
    carry_ref[...] = carry_new
    cnt_ref[...] = carry_new

    idx_out = jnp.zeros((tm, LANES), F32)
    gate_out = jnp.zeros((tm, LANES), F32)
    rank_out = jnp.zeros((tm, LANES), F32)
    for k in range(TOP_K):
        rk = jnp.sum(jnp.where(hots[k], before, 0.0), axis=-1, keepdims=True)
        idx_out = jnp.where(lanef == float(k), tops[k][1], idx_out)
        gate_out = jnp.where(lanef == float(k), es[k] / denom, gate_out)
        rank_out = jnp.where(lanef == float(k), rk, rank_out)
    idx_ref[...] = idx_out.astype(I32)
    gate_ref[...] = gate_out
    rank_ref[...] = rank_out.astype(I32)


def _moe_kernel(bea_ref, beb_ref, tok0_ref, tok1_ref, tok2_ref, tgtm_ref, tgt0_ref, tgt1_ref, h2p_hbm,
                w1a_ref, b1a_ref, w2a_ref, b2a_ref, w1b_ref, b1b_ref, w2b_ref, b2b_ref,
                out_hbm, xa, xb, oa, ob, gsem, ssem):
    g = pl.program_id(0)
    ng = pl.num_programs(0)
    rows = xa.shape[0] - 1
    nslab = xa.shape[1]
    half = nslab * LANES
    ff = w2a_ref.shape[1]
    cw = MXU_COLS
    n1, n2 = 2 * ff // cw, 2 * half // cw
    per = -(-rows // (n1 + n2))

    def row_copy(src, s, dst, d, sem):
        return pltpu.make_async_copy(src.at[s], dst.at[d], sem)

    def wait_gather(xbuf, sem):
        pltpu.make_async_copy(h2p_hbm.at[pl.ds(0, rows)], xbuf.at[pl.ds(0, rows)], sem).wait()

    def wait_scatter(obuf, sem):
        pltpu.make_async_copy(obuf.at[pl.ds(0, rows)], out_hbm.at[pl.ds(0, rows)], sem).wait()

    def expert(xbuf, w1_ref, b1_ref, w2_ref, b2_ref, obuf, tok_ref, xnext, gs, oprev, tgt_ref, ss):
        def issue(c):
            for r in range(c * per, min((c + 1) * per, rows)):
                row_copy(h2p_hbm, tok_ref[0, 0, r], xnext, r, gs).start()
                row_copy(oprev, r, out_hbm, tgt_ref[0, 0, r], ss).start()
            spare_a, spare_b = bea_ref[ng], beb_ref[ng]
            oprev[spare_a] = xnext[spare_a]
            z = lax.bitcast_convert_type((oprev[spare_b] >> 16) >> 16, F32)[0:1, :]
            return jnp.concatenate([z] * (cw // LANES), axis=1)

        xw = jnp.concatenate([xbuf[0:rows, j, :] for j in range(nslab)], axis=1)
        lo, hi = _unpack_bf16_pair(xw)
        lo, hi = lo.astype(BF16), hi.astype(BF16)
        hm = []
        for c in range(n1):
            cs = slice(c * cw, (c + 1) * cw)
            hm.append(_dot(lo, w1_ref[0, :half, cs]) + _dot(hi, w1_ref[0, half:, cs]) + (b1_ref[0, :, cs] + issue(c)))
        acts = []
        for c in range(n1 // 2):
            glu = jnp.minimum(hm[c], SWIGLU_LIMIT)
            lin = jnp.clip(hm[n1 // 2 + c], -SWIGLU_LIMIT, SWIGLU_LIMIT)
            acts.append((glu * jax.nn.sigmoid(SWIGLU_ALPHA * glu) * (lin + 1.0)).astype(BF16))
        act = jnp.concatenate(acts, axis=1)
        o = []
        for c in range(n2):
            cs = slice(c * cw, (c + 1) * cw)
            o.append(_dot(act, w2_ref[0, :, cs]) + (b2_ref[0, :, cs] + issue(n1 + c)))
        ow = _pack_bf16_pair(jnp.concatenate(o[:n2 // 2], axis=1), jnp.concatenate(o[n2 // 2:], axis=1))
        for j in range(nslab):
            obuf[0:rows, j, :] = ow[:, j * LANES:(j + 1) * LANES]

    @pl.when(g == 0)
    def _():
        xa[...] = jnp.zeros_like(xa)
        xb[...] = jnp.zeros_like(xb)
        oa[...] = jnp.zeros_like(oa)
        ob[...] = jnp.zeros_like(ob)

        def gather0(r, c):
            row_copy(h2p_hbm, tok0_ref[0, 0, r], xa, r, gsem.at[0]).start()
            return c

        lax.fori_loop(0, rows, gather0, 0)

    wait_gather(xa, gsem.at[0])

    @pl.when(g >= 1)
    def _():
        wait_scatter(oa, ssem.at[0])

    expert(xa, w1a_ref, b1a_ref, w2a_ref, b2a_ref, oa, tok1_ref, xb, gsem.at[1], ob, tgtm_ref, ssem.at[1])

    wait_gather(xb, gsem.at[1])
    wait_scatter(ob, ssem.at[1])

    expert(xb, w1b_ref, b1b_ref, w2b_ref, b2b_ref, ob, tok2_ref, xa, gsem.at[0], oa, tgt0_ref, ssem.at[0])

    @pl.when(g == ng - 1)
    def _():
        def scatter_last(r, c):
            row_copy(ob, r, out_hbm, tgt1_ref[0, 0, r], ssem.at[1]).start()
            return c

        wait_scatter(oa, ssem.at[0])
        lax.fori_loop(0, rows, scatter_last, 0)
        wait_scatter(ob, ssem.at[1])
        wait_gather(xa, gsem.at[0])


def _combine_kernel(x2_ref, o0_ref, o1_ref, o2_ref, o3_ref, gate_ref, out_ref):
    nslab = o0_ref.shape[1]
    half = nslab * LANES
    gates = [gate_ref[:, k:k + 1] for k in range(TOP_K)]
    for j in range(nslab):
        lo_sl = slice(j * LANES, (j + 1) * LANES)
        hi_sl = slice(half + j * LANES, half + (j + 1) * LANES)
        acc_lo = x2_ref[:, lo_sl]
        acc_hi = x2_ref[:, hi_sl]
        for k, o_ref in enumerate((o0_ref, o1_ref, o2_ref, o3_ref)):
            lo, hi = _unpack_bf16_pair(o_ref[:, j, :])
            acc_lo = acc_lo + gates[k] * lo
            acc_hi = acc_hi + gates[k] * hi
        out_ref[:, lo_sl] = acc_lo
        out_ref[:, hi_sl] = acc_hi


def _full(shape):
    return pl.BlockSpec(shape, lambda *_: (0,) * len(shape))


def _params(*semantics):
    return pltpu.CompilerParams(dimension_semantics=semantics, vmem_limit_bytes=VMEM_LIMIT)


def _memkv_call(mem, g_mem, w_mem_kv, g_k_cross):
    B, M, D = mem.shape
    return pl.pallas_call(
        _memkv_kernel,
        grid=(B,),
        in_specs=[pl.BlockSpec((1, M, D), lambda b: (b, 0, 0)), _full((1, D)), _full((D, 2 * CROSS_W)),
                  _full((1, CROSS_HEAD_DIM))],
        out_specs=[pl.BlockSpec((1, M, CROSS_W), lambda b: (b, 0, 0))] * 2,
        out_shape=[jax.ShapeDtypeStruct((B, M, CROSS_W), BF16)] * 2,
        compiler_params=_params("arbitrary"),
        name="memkv",
    )(mem, g_mem, w_mem_kv, g_k_cross)


def _stage1_call(x, pos, kc, vc, win, g_attn, wpool, pscale, gql, wuq, gkvl, wuk, wuv, gq, gk, invf, gqx, *, tm):
    B, S, D = x.shape
    M = kc.shape[1]
    tile = lambda w: pl.BlockSpec((1, tm, w), lambda b, i: (b, i, 0))
    widths = (POOL_W, MLA_HEADS * LANES, MLA_HEADS * LANES, MLA_HEADS * V_HEAD, CROSS_W, 3 * D)
    return pl.pallas_call(
        functools.partial(_stage1_kernel, tm=tm),
        grid=(B, S // tm),
        in_specs=[tile(D), tile(1),
                  pl.BlockSpec((1, M, CROSS_W), lambda b, i: (b, 0, 0)),
                  pl.BlockSpec((1, M, CROSS_W), lambda b, i: (b, 0, 0)),
                  _full(win.shape), _full((1, D)), _full(wpool.shape), _full((1, POOL_W)),
                  _full((1, Q_LORA)), _full(wuq.shape), _full((1, KV_LORA)), _full(wuk.shape), _full(wuv.shape),
                  _full((1, LANES)), _full((1, LANES)), _full((1, LANES)), _full((1, CROSS_HEAD_DIM))],
        out_specs=[tile(w) for w in widths],
        out_shape=[jax.ShapeDtypeStruct((B, S, w), BF16) for w in widths],
        scratch_shapes=[pltpu.VMEM((POOL_TAIL, POOL_W), F32)],
        compiler_params=_params("arbitrary", "arbitrary"),
        name="stage1",
    )(x, pos, kc, vc, win, g_attn, wpool, pscale, gql, wuq, gkvl, wuk, wuv, gq, gk, invf, gqx)


def _mla_call(q, k, v, *, tq):
    B, S, _ = q.shape
    return pl.pallas_call(
        functools.partial(_mla_kernel, tq=tq),
        grid=(B, MLA_HEADS // 2, S // tq),
        in_specs=[pl.BlockSpec((1, tq, 2 * LANES), lambda b, hp, i: (b, i, hp)),
                  pl.BlockSpec((1, S, 2 * LANES), lambda b, hp, i: (b, 0, hp)),
                  pl.BlockSpec((1, S, 2 * V_HEAD), lambda b, hp, i: (b, 0, hp))],
        out_specs=pl.BlockSpec((1, tq, 2 * V_HEAD), lambda b, hp, i: (b, i, hp)),
        out_shape=jax.ShapeDtypeStruct((B, S, MLA_HEADS * V_HEAD), BF16),
        compiler_params=_params("arbitrary", "arbitrary", "arbitrary"),
        name="mla",
    )(q, k, v)


def _stage3_call(ypool, omla, ocross, gates, x, wpo, wmo, wco, wo, g_ffn, wr, br, *, tm):
    T, D = x.shape
    trow = lambda w: pl.BlockSpec((tm, w), lambda i: (i, 0))
    return pl.pallas_call(
        functools.partial(_stage3_kernel, tm=tm),
        grid=(T // tm,),
        in_specs=[trow(POOL_W), trow(MLA_HEADS * V_HEAD), trow(CROSS_W), trow(3 * D), trow(D),
                  _full((POOL_W, D)), _full((MLA_HEADS * V_HEAD, D)), _full((CROSS_W, D)), _full((D, D)),
                  _full((1, D)), _full((D, LANES)), _full((1, LANES))],
        out_specs=[trow(D), pl.BlockSpec((tm, D // 2 // LANES, LANES), lambda i: (i, 0, 0)),
                   trow(LANES), trow(LANES), trow(LANES), _full((1, LANES))],
        out_shape=[jax.ShapeDtypeStruct((T, D), F32), jax.ShapeDtypeStruct((T, D // 2 // LANES, LANES), U32),
                   jax.ShapeDtypeStruct((T, LANES), I32), jax.ShapeDtypeStruct((T, LANES), F32),
                   jax.ShapeDtypeStruct((T, LANES), I32), jax.ShapeDtypeStruct((1, LANES), F32)],
        scratch_shapes=[pltpu.VMEM((1, LANES), F32)],
        compiler_params=_params("arbitrary"),
        name="stage3",
    )(ypool, omla, ocross, gates, x, wpo, wmo, wco, wo, g_ffn, wr, br)


def _moe_call(block_e, tok, tgt_shift, h2p, w1, b1, w2, b2):
    NB, _, R = tok.shape
    T, nslab, _ = h2p.shape
    _, D, FF2 = w1.shape
    smem = lambda imap: pl.BlockSpec((1, 1, R), imap, memory_space=pltpu.SMEM)

    def wspecs(which):
        imap = lambda g, ea, eb: ((ea, eb)[which][g], 0, 0)
        return [pl.BlockSpec((1, D, FF2), imap), pl.BlockSpec((1, 1, FF2), imap),
                pl.BlockSpec((1, FF2 // 2, D), imap), pl.BlockSpec((1, 1, D), imap)]

    return pl.pallas_call(
        _moe_kernel,
        grid_spec=pltpu.PrefetchScalarGridSpec(
            num_scalar_prefetch=2,
            grid=(NB // 2,),
            in_specs=[smem(lambda g, ea, eb: (0, 0, 0)),
                      smem(lambda g, ea, eb: (2 * g + 1, 0, 0)),
                      smem(lambda g, ea, eb: (jnp.minimum(2 * g + 2, NB - 1), 0, 0)),
                      smem(lambda g, ea, eb: (2 * g, 0, 0)),
                      smem(lambda g, ea, eb: (2 * g + 1, 0, 0)),
                      smem(lambda g, ea, eb: (2 * g + 2, 0, 0)),
                      pl.BlockSpec(memory_space=pl.ANY)] + wspecs(0) + wspecs(1),
            out_specs=pl.BlockSpec(memory_space=pl.ANY),
            scratch_shapes=[pltpu.VMEM((R + 1, nslab, LANES), U32)] * 4
                           + [pltpu.SemaphoreType.DMA((2,)), pltpu.SemaphoreType.DMA((2,))]),
        out_shape=jax.ShapeDtypeStruct((TOP_K * T + R, nslab, LANES), U32),
        compiler_params=_params("arbitrary"),
        name="moe",
    )(jnp.append(block_e[0::2], R), jnp.append(block_e[1::2], R), tok, tok, tok, tgt_shift, tgt_shift, tgt_shift,
      h2p, w1, b1, w2, b2, w1, b1, w2, b2)


def _combine_call(x2, out4, gate, *, tm):
    T, D = x2.shape
    nt = T // tm
    trow = lambda w: pl.BlockSpec((tm, w), lambda i: (i, 0))
    return pl.pallas_call(
        _combine_kernel,
        grid=(nt,),
        in_specs=[trow(D)]
                 + [pl.BlockSpec((tm,) + out4.shape[1:], lambda i, kk=kk: (kk * nt + i, 0, 0)) for kk in range(TOP_K)]
                 + [trow(LANES)],
        out_specs=trow(D),
        out_shape=jax.ShapeDtypeStruct((T, D), F32),
        compiler_params=_params("arbitrary"),
        name="combine",
    )(x2, out4, out4, out4, out4, gate)


def kernel(x, mem, positions, g_attn_norm, w_in, w_pool, pool_scale, g_q_lora, w_uq, g_kv_lora, w_ukv, g_q_mla, g_k_mla, g_mem_norm, w_mem_kv, g_q_cross, g_k_cross, w_pool_out, w_mla_out, w_cross_out, w_o, g_ffn_norm, w_router, b_router, w_exp_in, b_exp_in, w_exp_out, b_exp_out):
    B, S, D = x.shape
    T = B * S
    N = T * TOP_K
    E = N_EXPERTS
    R = ROW_BLOCK
    NB = (N + E * (R - 1) + R - 1) // R
    NB += NB % 2
    tm = min(512, S)
    tq = min(512, S)
    assert S % tm == 0 and T % tm == 0 and tm >= POOL_TAIL

    zpad = jnp.zeros((D, LANES - QK_ROPE), w_in.dtype)
    win = jnp.concatenate([w_in[:, :C_KPE + QK_ROPE], zpad, w_in[:, C_KPE + QK_ROPE:]], axis=1).astype(BF16)
    wuq = jnp.pad(w_uq.reshape(Q_LORA, MLA_HEADS, QK_HEAD), ((0, 0), (0, 0), (0, LANES - QK_HEAD)))
    wuq = wuq.reshape(Q_LORA, MLA_HEADS * LANES).astype(BF16)
    wukv = w_ukv.reshape(KV_LORA, MLA_HEADS, QK_NOPE + V_HEAD)
    wuk = jnp.pad(wukv[:, :, :QK_NOPE], ((0, 0), (0, 0), (0, LANES - QK_NOPE)))
    wuk = wuk.reshape(KV_LORA, MLA_HEADS * LANES).astype(BF16)
    wuv = wukv[:, :, QK_NOPE:].reshape(KV_LORA, MLA_HEADS * V_HEAD).astype(BF16)
    gq = jnp.pad(g_q_mla, (0, LANES - QK_HEAD)).reshape(1, LANES)
    gk = jnp.pad(g_k_mla, (0, LANES - QK_HEAD)).reshape(1, LANES)
    half = QK_ROPE // 2
    inv = ROPE_BASE ** (-jnp.arange(half, dtype=F32) / half)
    invf = jnp.zeros((LANES,), F32).at[QK_NOPE:QK_NOPE + half].set(inv).at[QK_NOPE + half:QK_HEAD].set(inv)
    invf = invf.reshape(1, LANES)
    wr = jnp.pad(w_router, ((0, 0), (0, LANES - E))).astype(BF16)
    br = jnp.concatenate([b_router, jnp.full((LANES - E,), NEG_BIG, F32)]).reshape(1, LANES)
    row = lambda v: v.reshape(1, -1)
    flat = lambda a: a.reshape(T, a.shape[-1])

    kc, vc = _memkv_call(mem, row(g_mem_norm), w_mem_kv.astype(BF16), row(g_k_cross))
    ypool, q, k, v, ocross, gates = _stage1_call(
        x, positions.reshape(B, S, 1), kc, vc, win, row(g_attn_norm), w_pool.astype(BF16),
        pool_scale.reshape(1, POOL_W), row(g_q_lora), wuq, row(g_kv_lora), wuk, wuv, gq, gk, invf,
        row(g_q_cross), tm=tm)
    omla = _mla_call(q, k, v, tq=tq)
    x2, h2p, idx, gate, rank, cnt = _stage3_call(
        flat(ypool), flat(omla), flat(ocross), flat(gates), flat(x), w_pool_out.astype(BF16),
        w_mla_out.astype(BF16), w_cross_out.astype(BF16), w_o.astype(BF16), row(g_ffn_norm), wr, br, tm=tm)

    counts = cnt[0, :E].astype(I32)
    padded = (counts + R - 1) // R * R
    pend = jnp.cumsum(padded)
    pstart = pend - padded
    dest = (pstart[idx[:, :TOP_K]] + rank[:, :TOP_K]).reshape(-1)
    row0 = jnp.arange(NB, dtype=I32) * R
    block_e = jnp.minimum(jnp.searchsorted(pend, row0, side='right'), E - 1).astype(I32)
    src = jnp.full((NB * R,), -1, I32).at[dest].set(jnp.arange(N, dtype=I32))
    dump = TOP_K * T + jnp.arange(NB * R, dtype=I32) % R
    tok = jnp.where(src >= 0, src // TOP_K, 0).reshape(NB, 1, R)
    tgt = jnp.where(src >= 0, (src % TOP_K) * T + src // TOP_K, dump)
    tgt_shift = jnp.concatenate([dump[:R], tgt]).reshape(NB + 1, 1, R)

    FF2 = w_exp_in.shape[-1]
    out4 = _moe_call(block_e, tok, tgt_shift, h2p, w_exp_in.astype(BF16), b_exp_in.reshape(E, 1, FF2),
                     w_exp_out.astype(BF16), b_exp_out.reshape(E, 1, D))
    return _combine_call(x2, out4, gate, tm=tm).reshape(B, S, D)
```

```python
import functools

import jax
import jax.numpy as jnp
from jax import lax
from jax.experimental import pallas as pl
from jax.experimental.pallas import tpu as pltpu

F32 = jnp.float32
BF16 = jnp.bfloat16
U32 = jnp.uint32
I32 = jnp.int32

EPS = 1e-6
ROPE_BASE = 10000.0
LANES = 128
SUBLANES = 8
MXU_COLS = 256
POOL_WINDOWS = (2, 4, 8, 16)
POOL_TAIL = 16
POOL_W = 512
MLA_HEADS = 8
QK_NOPE = 64
QK_ROPE = 32
QK_HEAD = QK_NOPE + QK_ROPE
V_HEAD = 64
Q_LORA = 384
KV_LORA = 256
CROSS_HEADS = 4
CROSS_HEAD_DIM = 128
CROSS_W = 512
N_EXPERTS = 32
TOP_K = 4
SWIGLU_LIMIT = 7.0
SWIGLU_ALPHA = 1.702
ROW_BLOCK = 512
LOG2E = 1.4426950408889634
NEG_BIG = -1e30

C_POOL = 0
C_QL = C_POOL + POOL_W
C_KVL = C_QL + Q_LORA
C_KPE = C_KVL + KV_LORA
C_XQ = C_KPE + LANES
C_GATE = C_XQ + CROSS_W

VMEM_LIMIT = 56 * 1024 * 1024


def _dot(a, b):
    return jnp.dot(a, b, preferred_element_type=F32)


def _dot_nt(a, b):
    return lax.dot_general(a, b, (((1,), (1,)), ((), ())), preferred_element_type=F32)


def _rms(xf, g, n=None):
    n = xf.shape[-1] if n is None else n
    ms = jnp.sum(xf * xf, axis=-1, keepdims=True) * (1.0 / n)
    return xf * lax.rsqrt(ms + EPS) * g


def _pack_bf16_pair(lo, hi):
    lo_b = lax.bitcast_convert_type(lo.astype(BF16).astype(F32), U32)
    hi_b = lax.bitcast_convert_type(hi.astype(BF16).astype(F32), U32)
    return (lo_b >> 16) | (hi_b & jnp.uint32(0xFFFF0000))


def _unpack_bf16_pair(w):
    lo = lax.bitcast_convert_type(w << 16, F32)
    hi = lax.bitcast_convert_type(w & jnp.uint32(0xFFFF0000), F32)
    return lo, hi


def _memkv_kernel(mem_ref, g_ref, w_ref, gk_ref, k_ref, v_ref):
    h = _rms(mem_ref[0], g_ref[...]).astype(BF16)
    kv = _dot(h, w_ref[...])
    for c in range(CROSS_HEADS):
        sl = slice(c * CROSS_HEAD_DIM, (c + 1) * CROSS_HEAD_DIM)
        k_ref[0, :, sl] = _rms(kv[:, sl], gk_ref[...]).astype(BF16)
    v_ref[0] = kv[:, CROSS_W:].astype(BF16)


def _stage1_kernel(x_ref, pos_ref, kc_ref, vc_ref, win_ref, gattn_ref, wpool_ref, pscale_ref,
                   gql_ref, wuq_ref, gkvl_ref, wuk_ref, wuv_ref, gq_ref, gk_ref, invf_ref, gqx_ref,
                   ypool_ref, q_ref, k_ref, v_ref, oc_ref, gates_ref, tail_ref, *, tm):
    i = pl.program_id(1)

    @pl.when(i == 0)
    def _():
        tail_ref[...] = jnp.zeros_like(tail_ref)

    h = _rms(x_ref[0], gattn_ref[...]).astype(BF16)

    u = _dot(h, win_ref[:, C_POOL:C_QL])
    ext = jnp.concatenate([tail_ref[...], u], axis=0)
    tail_ref[...] = u[tm - POOL_TAIL:, :]
    tpos = (i * tm + lax.broadcasted_iota(I32, (tm, 1), 0) + 1).astype(F32)
    run = ext
    shift = 1
    for g, win in enumerate(POOL_WINDOWS):
        while shift < win:
            run = run + pltpu.roll(run, shift, 0)
            shift *= 2
        sl = slice(g * LANES, (g + 1) * LANES)
        cnt = jnp.minimum(tpos, float(win))
        mixed = run[POOL_TAIL:, sl] / cnt - u[:, sl]
        y = _dot(mixed.astype(BF16), wpool_ref[g]) * pscale_ref[:, sl]
        ypool_ref[0, :, sl] = y.astype(BF16)

    lane = lax.broadcasted_iota(I32, (1, LANES), 1)
    ang = pos_ref[0].astype(F32) * invf_ref[...]
    cosf = jnp.cos(ang)
    sinf = jnp.sin(ang)
    half = QK_ROPE // 2
    sin_a = jnp.where((lane >= QK_NOPE) & (lane < QK_NOPE + half), -sinf, 0.0)
    sin_b = jnp.where((lane >= QK_NOPE + half) & (lane < QK_HEAD), sinf, 0.0)

    def rope(t):
        return t * cosf + pltpu.roll(t, LANES - half, 1) * sin_a + pltpu.roll(t, half, 1) * sin_b

    cq = _dot(h, win_ref[:, C_QL:C_KVL])
    qf = _dot(_rms(cq, gql_ref[...]).astype(BF16), wuq_ref[...])
    qscale = QK_HEAD ** -0.5 * LOG2E
    for hh in range(MLA_HEADS):
        sl = slice(hh * LANES, (hh + 1) * LANES)
        qn = _rms(qf[:, sl], gq_ref[...], QK_HEAD)
        q_ref[0, :, sl] = (rope(qn) * qscale).astype(BF16)

    ckv = _dot(h, win_ref[:, C_KVL:C_KPE])
    ckvn = _rms(ckv, gkvl_ref[...]).astype(BF16)
    kf = _dot(ckvn, wuk_ref[...])
    kpe = pltpu.roll(_dot(h, win_ref[:, C_KPE:C_XQ]), QK_NOPE, 1)
    for hh in range(MLA_HEADS):
        sl = slice(hh * LANES, (hh + 1) * LANES)
        kn = _rms(kf[:, sl] + kpe, gk_ref[...], QK_HEAD)
        k_ref[0, :, sl] = rope(kn).astype(BF16)
    v_ref[0] = _dot(ckvn, wuv_ref[...]).astype(BF16)

    qx = _dot(h, win_ref[:, C_XQ:C_GATE])
    xscale = CROSS_HEAD_DIM ** -0.5 * LOG2E
    for c in range(CROSS_HEADS):
        sl = slice(c * CROSS_HEAD_DIM, (c + 1) * CROSS_HEAD_DIM)
        qc = (_rms(qx[:, sl], gqx_ref[...]) * xscale).astype(BF16)
        s = _dot_nt(qc, kc_ref[0, :, sl])
        p = jnp.exp2(s - jnp.max(s, axis=-1, keepdims=True))
        o = _dot(p.astype(BF16), vc_ref[0, :, sl]) / jnp.sum(p, axis=-1, keepdims=True)
        oc_ref[0, :, sl] = o.astype(BF16)

    d = x_ref.shape[-1]
    for g in range(3):
        zg = _dot(h, win_ref[:, C_GATE + g * d:C_GATE + (g + 1) * d])
        gates_ref[0, :, g * d:(g + 1) * d] = jax.nn.sigmoid(zg).astype(BF16)


def _mla_kernel(q_ref, k_ref, v_ref, o_ref, *, tq):
    i = pl.program_id(2)
    lane = lax.broadcasted_iota(I32, (1, LANES), 1)
    row = lax.broadcasted_iota(I32, (tq, tq), 0)
    col = lax.broadcasted_iota(I32, (tq, tq), 1)
    qs = [q_ref[0, :, hh * LANES:(hh + 1) * LANES] for hh in range(2)]

    def step(j, carry, masked):
        off = pl.multiple_of(j * tq, tq)
        vt = v_ref[0, pl.ds(off, tq), :]
        new = []
        for hh in range(2):
            m, l, acc = carry[hh]
            kt = k_ref[0, pl.ds(off, tq), hh * LANES:(hh + 1) * LANES]
            s = _dot_nt(qs[hh], kt)
            if masked:
                s = jnp.where(col <= row, s, NEG_BIG)
            m_new = jnp.maximum(m, jnp.max(s, axis=-1, keepdims=True))
            a = jnp.exp2(m - m_new)
            p = jnp.exp2(s - m_new)
            l = a * l + jnp.sum(p, axis=-1, keepdims=True)
            acc = a * acc + _dot(p.astype(BF16), vt)
            new.append((m_new, l, acc))
        return tuple(new)

    init = (jnp.full((tq, 1), NEG_BIG, F32), jnp.zeros((tq, 1), F32), jnp.zeros((tq, LANES), F32))
    carry = lax.fori_loop(0, i, functools.partial(step, masked=False), (init, init))
    (_, l0, acc0), (_, l1, acc1) = step(i, carry, True)
    o_ref[0] = jnp.where(lane < V_HEAD, acc0 / l0, acc1 / l1).astype(BF16)


def _stage3_kernel(yp_ref, om_ref, oc_ref, gates_ref, x_ref, wpo_ref, wmo_ref, wco_ref, wo_ref,
                   gffn_ref, wr_ref, br_ref,
                   x2_ref, h2p_ref, idx_ref, gate_ref, rank_ref, cnt_ref, carry_ref, *, tm):
    i = pl.program_id(0)

    @pl.when(i == 0)
    def _():
        carry_ref[...] = jnp.zeros_like(carry_ref)

    d = x_ref.shape[-1]
    merged = (gates_ref[:, 0:d].astype(F32) * _dot(yp_ref[...], wpo_ref[...])
              + gates_ref[:, d:2 * d].astype(F32) * _dot(om_ref[...], wmo_ref[...])
              + gates_ref[:, 2 * d:3 * d].astype(F32) * _dot(oc_ref[...], wco_ref[...]))
    x2 = x_ref[...] + _dot(merged.astype(BF16), wo_ref[...])
    x2_ref[...] = x2
    h2 = _rms(x2, gffn_ref[...])
    h2p = _pack_bf16_pair(h2[:, :d // 2], h2[:, d // 2:])
    nslab = h2p_ref.shape[0] // tm
    for j in range(nslab):
        h2p_ref[pl.ds(j, tm, stride=nslab), :] = h2p[:, j * LANES:(j + 1) * LANES]

    logits = _dot(h2.astype(BF16), wr_ref[...]) + br_ref[...]
    lanef = lax.broadcasted_iota(I32, (tm, LANES), 1).astype(F32)
    rem = logits
    tops, hots = [], []
    for _ in range(TOP_K):
        m = jnp.max(rem, axis=-1, keepdims=True)
        ix = jnp.min(jnp.where(rem == m, lanef, float(LANES)), axis=-1, keepdims=True)
        hot = lanef == ix
        tops.append((m, ix))
        hots.append(hot)
        rem = jnp.where(hot, NEG_BIG * 2, rem)
    es = [jnp.exp(m - tops[0][0]) for m, _ in tops]
    denom = es[0] + es[1] + es[2] + es[3]

    onehot = jnp.zeros((tm, LANES), F32)
    for hot in hots:
        onehot = onehot + hot.astype(F32)
    r = lax.broadcasted_iota(I32, (tm, tm), 0)
    c = lax.broadcasted_iota(I32, (tm, tm), 1)
    tri = (c < r).astype(BF16)
    before = _dot(tri, onehot.astype(BF16)) + carry_ref[...]
    carry_new = carry_ref[...] + jnp.sum(onehot, axis=0, keepdims=True)
    carry_ref[...] = carry_new
    cnt_ref[...] = carry_new

    idx_out = jnp.zeros((tm, LANES), F32)
    gate_out = jnp.zeros((tm, LANES), F32)
    rank_out = jnp.zeros((tm, LANES), F32)
    for k in range(TOP_K):
        rk = jnp.sum(jnp.where(hots[k], before, 0.0), axis=-1, keepdims=True)
        idx_out = jnp.where(lanef == float(k), tops[k][1], idx_out)
        gate_out = jnp.where(lanef == float(k), es[k] / denom, gate_out)
        rank_out = jnp.where(lanef == float(k), rk, rank_out)
    idx_ref[...] = idx_out.astype(I32)
    gate_ref[...] = gate_out
    rank_ref[...] = rank_out.astype(I32)


def _moe_kernel(bea_ref, beb_ref, tok0_ref, tok1_ref, tok2_ref, tgtm_ref, tgt0_ref, tgt1_ref, h2p_hbm,
                w1a_ref, b1a_ref, w2a_ref, b2a_ref, w1b_ref, b1b_ref, w2b_ref, b2b_ref,
                out_hbm, xa, xb, oa, ob, gsem, ssem):
    g = pl.program_id(0)
    ng = pl.num_programs(0)
    nslab = h2p_hbm.shape[1]
    rows = xa.shape[0] // nslab - 2
    half = nslab * LANES
    ff = w2a_ref.shape[1]
    cw = MXU_COLS
    n1, n2 = 2 * ff // cw, 2 * half // cw
    per = -(-rows // (n1 + n2))

    def slab(buf, r):
        return buf.at[pl.ds(r * nslab, nslab), :]

    def wait_gather(sem):
        pltpu.make_async_copy(h2p_hbm.at[pl.ds(0, rows)], h2p_hbm.at[pl.ds(0, rows)], sem).wait()

    def wait_scatter(sem):
        pltpu.make_async_copy(out_hbm.at[pl.ds(0, rows)], out_hbm.at[pl.ds(0, rows)], sem).wait()

    def expert(xbuf, w1_ref, b1_ref, w2_ref, b2_ref, obuf, tok_ref, xnext, gs, oprev, tgt_ref, ss):
        def issue(c):
            for r in range(c * per, min((c + 1) * per, rows)):
                pltpu.make_async_copy(h2p_hbm.at[tok_ref[0, 0, r]], slab(xnext, r), gs).start()
                pltpu.make_async_copy(slab(oprev, r), out_hbm.at[tgt_ref[0, 0, r]], ss).start()
            spare_a = pl.multiple_of(bea_ref[ng], SUBLANES)
            spare_b = pl.multiple_of(beb_ref[ng], SUBLANES)
            oprev[pl.ds(spare_a, SUBLANES), :] = xnext[pl.ds(spare_a, SUBLANES), :]
            z = lax.bitcast_convert_type((oprev[pl.ds(spare_b, SUBLANES), :] >> 16) >> 16, F32)[0:1, :]
            return jnp.concatenate([z] * (cw // LANES), axis=1)

        xw = jnp.concatenate([xbuf[pl.ds(j, rows, stride=nslab), :] for j in range(nslab)], axis=1)
        lo, hi = _unpack_bf16_pair(xw)
        lo, hi = lo.astype(BF16), hi.astype(BF16)
        hm = []
        for c in range(n1):
            cs = slice(c * cw, (c + 1) * cw)
            hm.append(_dot(lo, w1_ref[0, :half, cs]) + _dot(hi, w1_ref[0, half:, cs]) + (b1_ref[0, :, cs] + issue(c)))
        acts = []
        for c in range(n1 // 2):
            glu = jnp.minimum(hm[c], SWIGLU_LIMIT)
            lin = jnp.clip(hm[n1 // 2 + c], -SWIGLU_LIMIT, SWIGLU_LIMIT)
            acts.append((glu * jax.nn.sigmoid(SWIGLU_ALPHA * glu) * (lin + 1.0)).astype(BF16))
        act = jnp.concatenate(acts, axis=1)
        o = []
        for c in range(n2):
            cs = slice(c * cw, (c + 1) * cw)
            o.append(_dot(act, w2_ref[0, :, cs]) + (b2_ref[0, :, cs] + issue(n1 + c)))
        ow = _pack_bf16_pair(jnp.concatenate(o[:n2 // 2], axis=1), jnp.concatenate(o[n2 // 2:], axis=1))
        for j in range(nslab):
            obuf[pl.ds(j, rows, stride=nslab), :] = ow[:, j * LANES:(j + 1) * LANES]

    @pl.when(g == 0)
    def _():
        xa[...] = jnp.zeros_like(xa)
        xb[...] = jnp.zeros_like(xb)
        oa[...] = jnp.zeros_like(oa)
        ob[...] = jnp.zeros_like(ob)

        def gather0(r, c):
            pltpu.make_async_copy(h2p_hbm.at[tok0_ref[0, 0, r]], slab(xa, r), gsem.at[0]).start()
            return c

        lax.fori_loop(0, rows, gather0, 0)

    wait_gather(gsem.at[0])

    @pl.when(g >= 1)
    def _():
        wait_scatter(ssem.at[0])

    expert(xa, w1a_ref, b1a_ref, w2a_ref, b2a_ref, oa, tok1_ref, xb, gsem.at[1], ob, tgtm_ref, ssem.at[1])

    wait_gather(gsem.at[1])
    wait_scatter(ssem.at[1])

    expert(xb, w1b_ref, b1b_ref, w2b_ref, b2b_ref, ob, tok2_ref, xa, gsem.at[0], oa, tgt0_ref, ssem.at[0])

    @pl.when(g == ng - 1)
    def _():
        def scatter_last(r, c):
            pltpu.make_async_copy(slab(ob, r), out_hbm.at[tgt1_ref[0, 0, r]], ssem.at[1]).start()
            return c

        wait_scatter(ssem.at[0])
        lax.fori_loop(0, rows, scatter_last, 0)
        wait_scatter(ssem.at[1])
        wait_gather(gsem.at[0])


def _combine_kernel(x2_ref, o0_ref, o1_ref, o2_ref, o3_ref, gate_ref, out_ref):
    tm = x2_ref.shape[0]
    nslab = o0_ref.shape[0] // tm
    half = nslab * LANES
    gates = [gate_ref[:, k:k + 1] for k in range(TOP_K)]
    for j in range(nslab):
        lo_sl = slice(j * LANES, (j + 1) * LANES)
        hi_sl = slice(half + j * LANES, half + (j + 1) * LANES)
        acc_lo = x2_ref[:, lo_sl]
        acc_hi = x2_ref[:, hi_sl]
        for k, o_ref in enumerate((o0_ref, o1_ref, o2_ref, o3_ref)):
            lo, hi = _unpack_bf16_pair(o_ref[pl.ds(j, tm, stride=nslab), :])
            acc_lo = acc_lo + gates[k] * lo
            acc_hi = acc_hi + gates[k] * hi
        out_ref[:, lo_sl] = acc_lo
        out_ref[:, hi_sl] = acc_hi


def _full(shape):
    return pl.BlockSpec(shape, lambda *_: (0,) * len(shape))


def _params(*semantics):
    return pltpu.CompilerParams(dimension_semantics=semantics, vmem_limit_bytes=VMEM_LIMIT)


def _memkv_call(mem, g_mem, w_mem_kv, g_k_cross):
    B, M, D = mem.shape
    return pl.pallas_call(
        _memkv_kernel,
        grid=(B,),
        in_specs=[pl.BlockSpec((1, M, D), lambda b: (b, 0, 0)), _full((1, D)), _full((D, 2 * CROSS_W)),
                  _full((1, CROSS_HEAD_DIM))],
        out_specs=[pl.BlockSpec((1, M, CROSS_W), lambda b: (b, 0, 0))] * 2,
        out_shape=[jax.ShapeDtypeStruct((B, M, CROSS_W), BF16)] * 2,
        compiler_params=_params("arbitrary"),
        name="memkv",
    )(mem, g_mem, w_mem_kv, g_k_cross)


def _stage1_call(x, pos, kc, vc, win, g_attn, wpool, pscale, gql, wuq, gkvl, wuk, wuv, gq, gk, invf, gqx, *, tm):
    B, S, D = x.shape
    M = kc.shape[1]
    tile = lambda w: pl.BlockSpec((1, tm, w), lambda b, i: (b, i, 0))
    widths = (POOL_W, MLA_HEADS * LANES, MLA_HEADS * LANES, MLA_HEADS * V_HEAD, CROSS_W, 3 * D)
    return pl.pallas_call(
        functools.partial(_stage1_kernel, tm=tm),
        grid=(B, S // tm),
        in_specs=[tile(D), tile(1),
                  pl.BlockSpec((1, M, CROSS_W), lambda b, i: (b, 0, 0)),
                  pl.BlockSpec((1, M, CROSS_W), lambda b, i: (b, 0, 0)),
                  _full(win.shape), _full((1, D)), _full(wpool.shape), _full((1, POOL_W)),
                  _full((1, Q_LORA)), _full(wuq.shape), _full((1, KV_LORA)), _full(wuk.shape), _full(wuv.shape),
                  _full((1, LANES)), _full((1, LANES)), _full((1, LANES)), _full((1, CROSS_HEAD_DIM))],
        out_specs=[tile(w) for w in widths],
        out_shape=[jax.ShapeDtypeStruct((B, S, w), BF16) for w in widths],
        scratch_shapes=[pltpu.VMEM((POOL_TAIL, POOL_W), F32)],
        compiler_params=_params("arbitrary", "arbitrary"),
        name="stage1",
    )(x, pos, kc, vc, win, g_attn, wpool, pscale, gql, wuq, gkvl, wuk, wuv, gq, gk, invf, gqx)


def _mla_call(q, k, v, *, tq):
    B, S, _ = q.shape
    return pl.pallas_call(
        functools.partial(_mla_kernel, tq=tq),
        grid=(B, MLA_HEADS // 2, S // tq),
        in_specs=[pl.BlockSpec((1, tq, 2 * LANES), lambda b, hp, i: (b, i, hp)),
                  pl.BlockSpec((1, S, 2 * LANES), lambda b, hp, i: (b, 0, hp)),
                  pl.BlockSpec((1, S, 2 * V_HEAD), lambda b, hp, i: (b, 0, hp))],
        out_specs=pl.BlockSpec((1, tq, 2 * V_HEAD), lambda b, hp, i: (b, i, hp)),
        out_shape=jax.ShapeDtypeStruct((B, S, MLA_HEADS * V_HEAD), BF16),
        compiler_params=_params("arbitrary", "arbitrary", "arbitrary"),
        name="mla",
    )(q, k, v)


def _stage3_call(ypool, omla, ocross, gates, x, wpo, wmo, wco, wo, g_ffn, wr, br, *, tm):
    T, D = x.shape
    trow = lambda w: pl.BlockSpec((tm, w), lambda i: (i, 0))
    return pl.pallas_call(
        functools.partial(_stage3_kernel, tm=tm),
        grid=(T // tm,),
        in_specs=[trow(POOL_W), trow(MLA_HEADS * V_HEAD), trow(CROSS_W), trow(3 * D), trow(D),
                  _full((POOL_W, D)), _full((MLA_HEADS * V_HEAD, D)), _full((CROSS_W, D)), _full((D, D)),
                  _full((1, D)), _full((D, LANES)), _full((1, LANES))],
        out_specs=[trow(D), pl.BlockSpec((tm * (D // 2 // LANES), LANES), lambda i: (i, 0)),
                   trow(LANES), trow(LANES), trow(LANES), _full((1, LANES))],
        out_shape=[jax.ShapeDtypeStruct((T, D), F32), jax.ShapeDtypeStruct((T * (D // 2 // LANES), LANES), U32),
                   jax.ShapeDtypeStruct((T, LANES), I32), jax.ShapeDtypeStruct((T, LANES), F32),
                   jax.ShapeDtypeStruct((T, LANES), I32), jax.ShapeDtypeStruct((1, LANES), F32)],
        scratch_shapes=[pltpu.VMEM((1, LANES), F32)],
        compiler_params=_params("arbitrary"),
        name="stage3",
    )(ypool, omla, ocross, gates, x, wpo, wmo, wco, wo, g_ffn, wr, br)


def _moe_call(block_e, tok, tgt_shift, h2p, w1, b1, w2, b2):
    NB, _, R = tok.shape
    T, nslab, _ = h2p.shape
    _, D, FF2 = w1.shape
    smem = lambda imap: pl.BlockSpec((1, 1, R), imap, memory_space=pltpu.SMEM)

    def wspecs(which):
        imap = lambda g, ea, eb: ((ea, eb)[which][g], 0, 0)
        return [pl.BlockSpec((1, D, FF2), imap), pl.BlockSpec((1, 1, FF2), imap),
                pl.BlockSpec((1, FF2 // 2, D), imap), pl.BlockSpec((1, 1, D), imap)]

    return pl.pallas_call(
        _moe_kernel,
        grid_spec=pltpu.PrefetchScalarGridSpec(
            num_scalar_prefetch=2,
            grid=(NB // 2,),
            in_specs=[smem(lambda g, ea, eb: (0, 0, 0)),
                      smem(lambda g, ea, eb: (2 * g + 1, 0, 0)),
                      smem(lambda g, ea, eb: (jnp.minimum(2 * g + 2, NB - 1), 0, 0)),
                      smem(lambda g, ea, eb: (2 * g, 0, 0)),
                      smem(lambda g, ea, eb: (2 * g + 1, 0, 0)),
                      smem(lambda g, ea, eb: (2 * g + 2, 0, 0)),
                      pl.BlockSpec(memory_space=pl.ANY)] + wspecs(0) + wspecs(1),
            out_specs=pl.BlockSpec(memory_space=pl.ANY),
            scratch_shapes=[pltpu.VMEM(((R + 2) * nslab, LANES), U32)] * 4
                           + [pltpu.SemaphoreType.DMA((2,)), pltpu.SemaphoreType.DMA((2,))]),
        out_shape=jax.ShapeDtypeStruct((TOP_K * T + R, nslab, LANES), U32),
        compiler_params=_params("arbitrary"),
        name="moe",
    )(jnp.append(block_e[0::2], R * nslab), jnp.append(block_e[1::2], R * nslab), tok, tok, tok, tgt_shift, tgt_shift, tgt_shift,
      h2p, w1, b1, w2, b2, w1, b1, w2, b2)


def _combine_call(x2, out4, gate, *, tm):
    T, D = x2.shape
    nt = T // tm
    nslab = out4.shape[1]
    out4 = out4.reshape(-1, LANES)
    trow = lambda w: pl.BlockSpec((tm, w), lambda i: (i, 0))
    return pl.pallas_call(
        _combine_kernel,
        grid=(nt,),
        in_specs=[trow(D)]
                 + [pl.BlockSpec((tm * nslab, LANES), lambda i, kk=kk: (kk * nt + i, 0)) for kk in range(TOP_K)]
                 + [trow(LANES)],
        out_specs=trow(D),
        out_shape=jax.ShapeDtypeStruct((T, D), F32),
        compiler_params=_params("arbitrary"),
        name="combine",
    )(x2, out4, out4, out4, out4, gate)


def kernel(x, mem, positions, g_attn_norm, w_in, w_pool, pool_scale, g_q_lora, w_uq, g_kv_lora, w_ukv, g_q_mla, g_k_mla, g_mem_norm, w_mem_kv, g_q_cross, g_k_cross, w_pool_out, w_mla_out, w_cross_out, w_o, g_ffn_norm, w_router, b_router, w_exp_in, b_exp_in, w_exp_out, b_exp_out):
    B, S, D = x.shape
    T = B * S
    N = T * TOP_K
    E = N_EXPERTS
    R = ROW_BLOCK
    NB = (N + E * (R - 1) + R - 1) // R
    NB += NB % 2
    tm = min(512, S)
    tq = min(512, S)
    assert S % tm == 0 and T % tm == 0 and tm >= POOL_TAIL

    zpad = jnp.zeros((D, LANES - QK_ROPE), w_in.dtype)
    win = jnp.concatenate([w_in[:, :C_KPE + QK_ROPE], zpad, w_in[:, C_KPE + QK_ROPE:]], axis=1).astype(BF16)
    wuq = jnp.pad(w_uq.reshape(Q_LORA, MLA_HEADS, QK_HEAD), ((0, 0), (0, 0), (0, LANES - QK_HEAD)))
    wuq = wuq.reshape(Q_LORA, MLA_HEADS * LANES).astype(BF16)
    wukv = w_ukv.reshape(KV_LORA, MLA_HEADS, QK_NOPE + V_HEAD)
    wuk = jnp.pad(wukv[:, :, :QK_NOPE], ((0, 0), (0, 0), (0, LANES - QK_NOPE)))
    wuk = wuk.reshape(KV_LORA, MLA_HEADS * LANES).astype(BF16)
    wuv = wukv[:, :, QK_NOPE:].reshape(KV_LORA, MLA_HEADS * V_HEAD).astype(BF16)
    gq = jnp.pad(g_q_mla, (0, LANES - QK_HEAD)).reshape(1, LANES)
    gk = jnp.pad(g_k_mla, (0, LANES - QK_HEAD)).reshape(1, LANES)
    half = QK_ROPE // 2
    inv = ROPE_BASE ** (-jnp.arange(half, dtype=F32) / half)
    invf = jnp.zeros((LANES,), F32).at[QK_NOPE:QK_NOPE + half].set(inv).at[QK_NOPE + half:QK_HEAD].set(inv)
    invf = invf.reshape(1, LANES)
    wr = jnp.pad(w_router, ((0, 0), (0, LANES - E))).astype(BF16)
    br = jnp.concatenate([b_router, jnp.full((LANES - E,), NEG_BIG, F32)]).reshape(1, LANES)
    row = lambda v: v.reshape(1, -1)
    flat = lambda a: a.reshape(T, a.shape[-1])

    kc, vc = _memkv_call(mem, row(g_mem_norm), w_mem_kv.astype(BF16), row(g_k_cross))
    ypool, q, k, v, ocross, gates = _stage1_call(
        x, positions.reshape(B, S, 1), kc, vc, win, row(g_attn_norm), w_pool.astype(BF16),
        pool_scale.reshape(1, POOL_W), row(g_q_lora), wuq, row(g_kv_lora), wuk, wuv, gq, gk, invf,
        row(g_q_cross), tm=tm)
    omla = _mla_call(q, k, v, tq=tq)
    x2, h2p, idx, gate, rank, cnt = _stage3_call(
        flat(ypool), flat(omla), flat(ocross), flat(gates), flat(x), w_pool_out.astype(BF16),
        w_mla_out.astype(BF16), w_cross_out.astype(BF16), w_o.astype(BF16), row(g_ffn_norm), wr, br, tm=tm)

    counts = cnt[0, :E].astype(I32)
    padded = (counts + R - 1) // R * R
    pend = jnp.cumsum(padded)
    pstart = pend - padded
    dest = (pstart[idx[:, :TOP_K]] + rank[:, :TOP_K]).reshape(-1)
    row0 = jnp.arange(NB, dtype=I32) * R
    block_e = jnp.minimum(jnp.searchsorted(pend, row0, side='right'), E - 1).astype(I32)
    src = jnp.full((NB * R,), -1, I32).at[dest].set(jnp.arange(N, dtype=I32))
    dump = TOP_K * T + jnp.arange(NB * R, dtype=I32) % R
    tok = jnp.where(src >= 0, src // TOP_K, 0).reshape(NB, 1, R)
    tgt = jnp.where(src >= 0, (src % TOP_K) * T + src // TOP_K, dump)
    tgt_shift = jnp.concatenate([dump[:R], tgt]).reshape(NB + 1, 1, R)

    FF2 = w_exp_in.shape[-1]
    out4 = _moe_call(block_e, tok, tgt_shift, h2p.reshape(T, -1, LANES), w_exp_in.astype(BF16), b_exp_in.reshape(E, 1, FF2),
                     w_exp_out.astype(BF16), b_exp_out.reshape(E, 1, D))
    return _combine_call(x2, out4, gate, tm=tm).reshape(B, S, D)
```

```python
import functools

import jax
import jax.numpy as jnp
from jax import lax
from jax.experimental import pallas as pl
from jax.experimental.pallas import tpu as pltpu

F32 = jnp.float32
BF16 = jnp.bfloat16
U32 = jnp.uint32
I32 = jnp.int32

EPS = 1e-6
ROPE_BASE = 10000.0
LANES = 128
SUBLANES = 8
MXU_COLS = 256
POOL_WINDOWS = (2, 4, 8, 16)
POOL_TAIL = 16
POOL_W = 512
MLA_HEADS = 8
QK_NOPE = 64
QK_ROPE = 32
QK_HEAD = QK_NOPE + QK_ROPE
V_HEAD = 64
Q_LORA = 384
KV_LORA = 256
CROSS_HEADS = 4
CROSS_HEAD_DIM = 128
CROSS_W = 512
N_EXPERTS = 32
TOP_K = 4
SWIGLU_LIMIT = 7.0
SWIGLU_ALPHA = 1.702
ROW_BLOCK = 512
LOG2E = 1.4426950408889634
NEG_BIG = -1e30

C_POOL = 0
C_QL = C_POOL + POOL_W
C_KVL = C_QL + Q_LORA
C_KPE = C_KVL + KV_LORA
C_XQ = C_KPE + LANES
C_GATE = C_XQ + CROSS_W

VMEM_LIMIT = 56 * 1024 * 1024


def _dot(a, b):
    return jnp.dot(a, b, preferred_element_type=F32)


def _dot_nt(a, b):
    return lax.dot_general(a, b, (((1,), (1,)), ((), ())), preferred_element_type=F32)


def _rms(xf, g, n=None):
    n = xf.shape[-1] if n is None else n
    ms = jnp.sum(xf * xf, axis=-1, keepdims=True) * (1.0 / n)
    return xf * lax.rsqrt(ms + EPS) * g


def _pack_bf16_pair(lo, hi):
    lo_b = lax.bitcast_convert_type(lo.astype(BF16).astype(F32), U32)
    hi_b = lax.bitcast_convert_type(hi.astype(BF16).astype(F32), U32)
    return (lo_b >> 16) | (hi_b & jnp.uint32(0xFFFF0000))


def _unpack_bf16_pair(w):
    lo = lax.bitcast_convert_type(w << 16, F32)
    hi = lax.bitcast_convert_type(w & jnp.uint32(0xFFFF0000), F32)
    return lo, hi


def _memkv_kernel(mem_ref, g_ref, w_ref, gk_ref, k_ref, v_ref):
    h = _rms(mem_ref[0], g_ref[...]).astype(BF16)
    kv = _dot(h, w_ref[...])
    for c in range(CROSS_HEADS):
        sl = slice(c * CROSS_HEAD_DIM, (c + 1) * CROSS_HEAD_DIM)
        k_ref[0, :, sl] = _rms(kv[:, sl], gk_ref[...]).astype(BF16)
    v_ref[0] = kv[:, CROSS_W:].astype(BF16)


def _stage1_kernel(x_ref, pos_ref, kc_ref, vc_ref, win_ref, gattn_ref, wpool_ref, pscale_ref,
                   gql_ref, wuq_ref, gkvl_ref, wuk_ref, wuv_ref, gq_ref, gk_ref, invf_ref, gqx_ref,
                   ypool_ref, q_ref, k_ref, v_ref, oc_ref, gates_ref, tail_ref, *, tm):
    i = pl.program_id(1)

    @pl.when(i == 0)
    def _():
        tail_ref[...] = jnp.zeros_like(tail_ref)

    h = _rms(x_ref[0], gattn_ref[...]).astype(BF16)

    u = _dot(h, win_ref[:, C_POOL:C_QL])
    ext = jnp.concatenate([tail_ref[...], u], axis=0)
    tail_ref[...] = u[tm - POOL_TAIL:, :]
    tpos = (i * tm + lax.broadcasted_iota(I32, (tm, 1), 0) + 1).astype(F32)
    run = ext
    shift = 1
    for g, win in enumerate(POOL_WINDOWS):
        while shift < win:
            run = run + pltpu.roll(run, shift, 0)
            shift *= 2
        sl = slice(g * LANES, (g + 1) * LANES)
        cnt = jnp.minimum(tpos, float(win))
        mixed = run[POOL_TAIL:, sl] / cnt - u[:, sl]
        y = _dot(mixed.astype(BF16), wpool_ref[g]) * pscale_ref[:, sl]
        ypool_ref[0, :, sl] = y.astype(BF16)

    lane = lax.broadcasted_iota(I32, (1, LANES), 1)
    ang = pos_ref[0].astype(F32) * invf_ref[...]
    cosf = jnp.cos(ang)
    sinf = jnp.sin(ang)
    half = QK_ROPE // 2
    sin_a = jnp.where((lane >= QK_NOPE) & (lane < QK_NOPE + half), -sinf, 0.0)
    sin_b = jnp.where((lane >= QK_NOPE + half) & (lane < QK_HEAD), sinf, 0.0)

    def rope(t):
        return t * cosf + pltpu.roll(t, LANES - half, 1) * sin_a + pltpu.roll(t, half, 1) * sin_b

    cq = _dot(h, win_ref[:, C_QL:C_KVL])
    qf = _dot(_rms(cq, gql_ref[...]).astype(BF16), wuq_ref[...])
    qscale = QK_HEAD ** -0.5 * LOG2E
    for hh in range(MLA_HEADS):
        sl = slice(hh * LANES, (hh + 1) * LANES)
        qn = _rms(qf[:, sl], gq_ref[...], QK_HEAD)
        q_ref[0, :, sl] = (rope(qn) * qscale).astype(BF16)

    ckv = _dot(h, win_ref[:, C_KVL:C_KPE])
    ckvn = _rms(ckv, gkvl_ref[...]).astype(BF16)
    kf = _dot(ckvn, wuk_ref[...])
    kpe = pltpu.roll(_dot(h, win_ref[:, C_KPE:C_XQ]), QK_NOPE, 1)
    for hh in range(MLA_HEADS):
        sl = slice(hh * LANES, (hh + 1) * LANES)
        kn = _rms(kf[:, sl] + kpe, gk_ref[...], QK_HEAD)
        k_ref[0, :, sl] = rope(kn).astype(BF16)
    v_ref[0] = _dot(ckvn, wuv_ref[...]).astype(BF16)

    qx = _dot(h, win_ref[:, C_XQ:C_GATE])
    xscale = CROSS_HEAD_DIM ** -0.5 * LOG2E
    for c in range(CROSS_HEADS):
        sl = slice(c * CROSS_HEAD_DIM, (c + 1) * CROSS_HEAD_DIM)
        qc = (_rms(qx[:, sl], gqx_ref[...]) * xscale).astype(BF16)
        s = _dot_nt(qc, kc_ref[0, :, sl])
        p = jnp.exp2(s - jnp.max(s, axis=-1, keepdims=True))
        o = _dot(p.astype(BF16), vc_ref[0, :, sl]) / jnp.sum(p, axis=-1, keepdims=True)
        oc_ref[0, :, sl] = o.astype(BF16)

    d = x_ref.shape[-1]
    for g in range(3):
        zg = _dot(h, win_ref[:, C_GATE + g * d:C_GATE + (g + 1) * d])
        gates_ref[0, :, g * d:(g + 1) * d] = jax.nn.sigmoid(zg).astype(BF16)


def _mla_kernel(q_ref, k_ref, v_ref, o_ref, *, tq):
    i = pl.program_id(2)
    lane = lax.broadcasted_iota(I32, (1, LANES), 1)
    row = lax.broadcasted_iota(I32, (tq, tq), 0)
    col = lax.broadcasted_iota(I32, (tq, tq), 1)
    qs = [q_ref[0, :, hh * LANES:(hh + 1) * LANES] for hh in range(2)]

    def step(j, carry, masked):
        off = pl.multiple_of(j * tq, tq)
        vt = v_ref[0, pl.ds(off, tq), :]
        new = []
        for hh in range(2):
            m, l, acc = carry[hh]
            kt = k_ref[0, pl.ds(off, tq), hh * LANES:(hh + 1) * LANES]
            s = _dot_nt(qs[hh], kt)
            if masked:
                s = jnp.where(col <= row, s, NEG_BIG)
            m_new = jnp.maximum(m, jnp.max(s, axis=-1, keepdims=True))
            a = jnp.exp2(m - m_new)
            p = jnp.exp2(s - m_new)
            l = a * l + jnp.sum(p, axis=-1, keepdims=True)
            acc = a * acc + _dot(p.astype(BF16), vt)
            new.append((m_new, l, acc))
        return tuple(new)

    init = (jnp.full((tq, 1), NEG_BIG, F32), jnp.zeros((tq, 1), F32), jnp.zeros((tq, LANES), F32))
    carry = lax.fori_loop(0, i, functools.partial(step, masked=False), (init, init))
    (_, l0, acc0), (_, l1, acc1) = step(i, carry, True)
    o_ref[0] = jnp.where(lane < V_HEAD, acc0 / l0, acc1 / l1).astype(BF16)


def _stage3_kernel(yp_ref, om_ref, oc_ref, gates_ref, x_ref, wpo_ref, wmo_ref, wco_ref, wo_ref,
                   gffn_ref, wr_ref, br_ref,
                   x2_ref, h2p_ref, idx_ref, gate_ref, rank_ref, cnt_ref, carry_ref, *, tm):
    i = pl.program_id(0)

    @pl.when(i == 0)
    def _():
        carry_ref[...] = jnp.zeros_like(carry_ref)

    d = x_ref.shape[-1]
    merged = (gates_ref[:, 0:d].astype(F32) * _dot(yp_ref[...], wpo_ref[...])
              + gates_ref[:, d:2 * d].astype(F32) * _dot(om_ref[...], wmo_ref[...])
              + gates_ref[:, 2 * d:3 * d].astype(F32) * _dot(oc_ref[...], wco_ref[...]))
    x2 = x_ref[...] + _dot(merged.astype(BF16), wo_ref[...])
    x2_ref[...] = x2
    h2 = _rms(x2, gffn_ref[...])
    h2p = _pack_bf16_pair(h2[:, :d // 2], h2[:, d // 2:])
    nslab = h2p_ref.shape[0] // tm
    for j in range(nslab):
        h2p_ref[pl.ds(j, tm, stride=nslab), :] = h2p[:, j * LANES:(j + 1) * LANES]

    logits = _dot(h2.astype(BF16), wr_ref[...]) + br_ref[...]
    lanef = lax.broadcasted_iota(I32, (tm, LANES), 1).astype(F32)
    rem = logits
    tops, hots = [], []
    for _ in range(TOP_K):
        m = jnp.max(rem, axis=-1, keepdims=True)
        ix = jnp.min(jnp.where(rem == m, lanef, float(LANES)), axis=-1, keepdims=True)
        hot = lanef == ix
        tops.append((m, ix))
        hots.append(hot)
        rem = jnp.where(hot, NEG_BIG * 2, rem)
    es = [jnp.exp(m - tops[0][0]) for m, _ in tops]
    denom = es[0] + es[1] + es[2] + es[3]

    onehot = jnp.zeros((tm, LANES), F32)
    for hot in hots:
        onehot = onehot + hot.astype(F32)
    r = lax.broadcasted_iota(I32, (tm, tm), 0)
    c = lax.broadcasted_iota(I32, (tm, tm), 1)
    tri = (c < r).astype(BF16)
    before = _dot(tri, onehot.astype(BF16)) + carry_ref[...]
    carry_new = carry_ref[...] + jnp.sum(onehot, axis=0, keepdims=True)
    carry_ref[...] = carry_new
    cnt_ref[...] = carry_new

    idx_out = jnp.zeros((tm, LANES), F32)
    gate_out = jnp.zeros((tm, LANES), F32)
    rank_out = jnp.zeros((tm, LANES), F32)
    for k in range(TOP_K):
        rk = jnp.sum(jnp.where(hots[k], before, 0.0), axis=-1, keepdims=True)
        idx_out = jnp.where(lanef == float(k), tops[k][1], idx_out)
        gate_out = jnp.where(lanef == float(k), es[k] / denom, gate_out)
        rank_out = jnp.where(lanef == float(k), rk, rank_out)
    idx_ref[...] = idx_out.astype(I32)
    gate_ref[...] = gate_out
    rank_ref[...] = rank_out.astype(I32)


def _moe_kernel(bea_ref, beb_ref, tok0_ref, tok1_ref, tok2_ref, tgtm_ref, tgt0_ref, tgt1_ref, h2p_hbm,
                w1a_ref, b1a_ref, w2a_ref, b2a_ref, w1b_ref, b1b_ref, w2b_ref, b2b_ref,
                out_hbm, xa, xb, oa, ob, gsem, ssem):
    g = pl.program_id(0)
    ng = pl.num_programs(0)
    nslab = h2p_hbm.shape[1]
    rows = xa.shape[0] // nslab - 2
    half = nslab * LANES
    ff = w2a_ref.shape[1]
    cw = MXU_COLS
    n1, n2 = 2 * ff // cw, 2 * half // cw
    per = -(-rows // (n1 + n2))

    def slab(buf, r):
        return buf.at[pl.ds(r * nslab, nslab), :]

    def wait_gather(sem):
        pltpu.make_async_copy(h2p_hbm.at[pl.ds(0, rows)], h2p_hbm.at[pl.ds(0, rows)], sem).wait()

    def wait_scatter(sem):
        pltpu.make_async_copy(out_hbm.at[pl.ds(0, rows)], out_hbm.at[pl.ds(0, rows)], sem).wait()

    def expert(xbuf, w1_ref, b1_ref, w2_ref, b2_ref, obuf, tok_ref, xnext, gs, oprev, tgt_ref, ss):
        def issue(c):
            for r in range(c * per, min((c + 1) * per, rows)):
                pltpu.make_async_copy(h2p_hbm.at[tok_ref[0, 0, r]], slab(xnext, r), gs).start()
                pltpu.make_async_copy(slab(oprev, r), out_hbm.at[tgt_ref[0, 0, r]], ss).start()
            spare_a = pl.multiple_of(bea_ref[ng], SUBLANES)
            spare_b = pl.multiple_of(beb_ref[ng], SUBLANES)
            oprev[pl.ds(spare_a, SUBLANES), :] = xnext[pl.ds(spare_a, SUBLANES), :]
            z = lax.bitcast_convert_type((oprev[pl.ds(spare_b, SUBLANES), :] >> 16) >> 16, F32)[0:1, :]
            return jnp.concatenate([z] * (cw // LANES), axis=1)

        xw = jnp.concatenate([xbuf[pl.ds(j, rows, stride=nslab), :] for j in range(nslab)], axis=1)
        lo, hi = _unpack_bf16_pair(xw)
        lo, hi = lo.astype(BF16), hi.astype(BF16)
        hm = []
        for c in range(n1):
            cs = slice(c * cw, (c + 1) * cw)
            hm.append(_dot(lo, w1_ref[0, :half, cs]) + _dot(hi, w1_ref[0, half:, cs]) + (b1_ref[0, :, cs] + issue(c)))
        acts = []
        for c in range(n1 // 2):
            glu = jnp.minimum(hm[c], SWIGLU_LIMIT)
            lin = jnp.clip(hm[n1 // 2 + c], -SWIGLU_LIMIT, SWIGLU_LIMIT)
            acts.append((glu * jax.nn.sigmoid(SWIGLU_ALPHA * glu) * (lin + 1.0)).astype(BF16))
        act = jnp.concatenate(acts, axis=1)
        o = []
        for c in range(n2):
            cs = slice(c * cw, (c + 1) * cw)
            o.append(_dot(act, w2_ref[0, :, cs]) + (b2_ref[0, :, cs] + issue(n1 + c)))
        ow = _pack_bf16_pair(jnp.concatenate(o[:n2 // 2], axis=1), jnp.concatenate(o[n2 // 2:], axis=1))
        for j in range(nslab):
            obuf[pl.ds(j, rows, stride=nslab), :] = ow[:, j * LANES:(j + 1) * LANES]

    @pl.when(g == 0)
    def _():
        xa[...] = jnp.zeros_like(xa)
        xb[...] = jnp.zeros_like(xb)
        oa[...] = jnp.zeros_like(oa)
        ob[...] = jnp.zeros_like(ob)

        def gather0(r, c):
            pltpu.make_async_copy(h2p_hbm.at[tok0_ref[0, 0, r]], slab(xa, r), gsem.at[0]).start()
            return c

        lax.fori_loop(0, rows, gather0, 0)

    last = bea_ref[ng + 1]

    @pl.when(g <= last)
    def _():
        wait_gather(gsem.at[0])

        @pl.when(g >= 1)
        def _():
            wait_scatter(ssem.at[0])

        expert(xa, w1a_ref, b1a_ref, w2a_ref, b2a_ref, oa, tok1_ref, xb, gsem.at[1], ob, tgtm_ref, ssem.at[1])

        wait_gather(gsem.at[1])
        wait_scatter(ssem.at[1])

        expert(xb, w1b_ref, b1b_ref, w2b_ref, b2b_ref, ob, tok2_ref, xa, gsem.at[0], oa, tgt0_ref, ssem.at[0])

    @pl.when(g == last)
    def _():
        def scatter_last(r, c):
            pltpu.make_async_copy(slab(ob, r), out_hbm.at[tgt1_ref[0, 0, r]], ssem.at[1]).start()
            return c

        wait_scatter(ssem.at[0])
        lax.fori_loop(0, rows, scatter_last, 0)
        wait_scatter(ssem.at[1])
        wait_gather(gsem.at[0])


def _combine_kernel(x2_ref, o0_ref, o1_ref, o2_ref, o3_ref, gate_ref, out_ref):
    tm = x2_ref.shape[0]
    nslab = o0_ref.shape[0] // tm
    half = nslab * LANES
    gates = [gate_ref[:, k:k + 1] for k in range(TOP_K)]
    for j in range(nslab):
        lo_sl = slice(j * LANES, (j + 1) * LANES)
        hi_sl = slice(half + j * LANES, half + (j + 1) * LANES)
        acc_lo = x2_ref[:, lo_sl]
        acc_hi = x2_ref[:, hi_sl]
        for k, o_ref in enumerate((o0_ref, o1_ref, o2_ref, o3_ref)):
            lo, hi = _unpack_bf16_pair(o_ref[pl.ds(j, tm, stride=nslab), :])
            acc_lo = acc_lo + gates[k] * lo
            acc_hi = acc_hi + gates[k] * hi
        out_ref[:, lo_sl] = acc_lo
        out_ref[:, hi_sl] = acc_hi


def _full(shape):
    return pl.BlockSpec(shape, lambda *_: (0,) * len(shape))


def _params(*semantics):
    return pltpu.CompilerParams(dimension_semantics=semantics, vmem_limit_bytes=VMEM_LIMIT)


def _memkv_call(mem, g_mem, w_mem_kv, g_k_cross):
    B, M, D = mem.shape
    return pl.pallas_call(
        _memkv_kernel,
        grid=(B,),
        in_specs=[pl.BlockSpec((1, M, D), lambda b: (b, 0, 0)), _full((1, D)), _full((D, 2 * CROSS_W)),
                  _full((1, CROSS_HEAD_DIM))],
        out_specs=[pl.BlockSpec((1, M, CROSS_W), lambda b: (b, 0, 0))] * 2,
        out_shape=[jax.ShapeDtypeStruct((B, M, CROSS_W), BF16)] * 2,
        compiler_params=_params("arbitrary"),
        name="memkv",
    )(mem, g_mem, w_mem_kv, g_k_cross)


def _stage1_call(x, pos, kc, vc, win, g_attn, wpool, pscale, gql, wuq, gkvl, wuk, wuv, gq, gk, invf, gqx, *, tm):
    B, S, D = x.shape
    M = kc.shape[1]
    tile = lambda w: pl.BlockSpec((1, tm, w), lambda b, i: (b, i, 0))
    widths = (POOL_W, MLA_HEADS * LANES, MLA_HEADS * LANES, MLA_HEADS * V_HEAD, CROSS_W, 3 * D)
    return pl.pallas_call(
        functools.partial(_stage1_kernel, tm=tm),
        grid=(B, S // tm),
        in_specs=[tile(D), tile(1),
                  pl.BlockSpec((1, M, CROSS_W), lambda b, i: (b, 0, 0)),
                  pl.BlockSpec((1, M, CROSS_W), lambda b, i: (b, 0, 0)),
                  _full(win.shape), _full((1, D)), _full(wpool.shape), _full((1, POOL_W)),
                  _full((1, Q_LORA)), _full(wuq.shape), _full((1, KV_LORA)), _full(wuk.shape), _full(wuv.shape),
                  _full((1, LANES)), _full((1, LANES)), _full((1, LANES)), _full((1, CROSS_HEAD_DIM))],
        out_specs=[tile(w) for w in widths],
        out_shape=[jax.ShapeDtypeStruct((B, S, w), BF16) for w in widths],
        scratch_shapes=[pltpu.VMEM((POOL_TAIL, POOL_W), F32)],
        compiler_params=_params("arbitrary", "arbitrary"),
        name="stage1",
    )(x, pos, kc, vc, win, g_attn, wpool, pscale, gql, wuq, gkvl, wuk, wuv, gq, gk, invf, gqx)


def _mla_call(q, k, v, *, tq):
    B, S, _ = q.shape
    return pl.pallas_call(
        functools.partial(_mla_kernel, tq=tq),
        grid=(B, MLA_HEADS // 2, S // tq),
        in_specs=[pl.BlockSpec((1, tq, 2 * LANES), lambda b, hp, i: (b, i, hp)),
                  pl.BlockSpec((1, S, 2 * LANES), lambda b, hp, i: (b, 0, hp)),
                  pl.BlockSpec((1, S, 2 * V_HEAD), lambda b, hp, i: (b, 0, hp))],
        out_specs=pl.BlockSpec((1, tq, 2 * V_HEAD), lambda b, hp, i: (b, i, hp)),
        out_shape=jax.ShapeDtypeStruct((B, S, MLA_HEADS * V_HEAD), BF16),
        compiler_params=_params("arbitrary", "arbitrary", "arbitrary"),
        name="mla",
    )(q, k, v)


def _stage3_call(ypool, omla, ocross, gates, x, wpo, wmo, wco, wo, g_ffn, wr, br, *, tm):
    T, D = x.shape
    trow = lambda w: pl.BlockSpec((tm, w), lambda i: (i, 0))
    return pl.pallas_call(
        functools.partial(_stage3_kernel, tm=tm),
        grid=(T // tm,),
        in_specs=[trow(POOL_W), trow(MLA_HEADS * V_HEAD), trow(CROSS_W), trow(3 * D), trow(D),
                  _full((POOL_W, D)), _full((MLA_HEADS * V_HEAD, D)), _full((CROSS_W, D)), _full((D, D)),
                  _full((1, D)), _full((D, LANES)), _full((1, LANES))],
        out_specs=[trow(D), pl.BlockSpec((tm * (D // 2 // LANES), LANES), lambda i: (i, 0)),
                   trow(LANES), trow(LANES), trow(LANES), _full((1, LANES))],
        out_shape=[jax.ShapeDtypeStruct((T, D), F32), jax.ShapeDtypeStruct((T * (D // 2 // LANES), LANES), U32),
                   jax.ShapeDtypeStruct((T, LANES), I32), jax.ShapeDtypeStruct((T, LANES), F32),
                   jax.ShapeDtypeStruct((T, LANES), I32), jax.ShapeDtypeStruct((1, LANES), F32)],
        scratch_shapes=[pltpu.VMEM((1, LANES), F32)],
        compiler_params=_params("arbitrary"),
        name="stage3",
    )(ypool, omla, ocross, gates, x, wpo, wmo, wco, wo, g_ffn, wr, br)


def _moe_call(block_e, last_step, tok, tgt_shift, h2p, w1, b1, w2, b2):
    NB, _, R = tok.shape
    T, nslab, _ = h2p.shape
    tail = jnp.stack([jnp.int32(R * nslab), last_step.astype(I32)])
    _, D, FF2 = w1.shape
    smem = lambda imap: pl.BlockSpec((1, 1, R), imap, memory_space=pltpu.SMEM)

    def wspecs(which):
        imap = lambda g, ea, eb: ((ea, eb)[which][g], 0, 0)
        return [pl.BlockSpec((1, D, FF2), imap), pl.BlockSpec((1, 1, FF2), imap),
                pl.BlockSpec((1, FF2 // 2, D), imap), pl.BlockSpec((1, 1, D), imap)]

    return pl.pallas_call(
        _moe_kernel,
        grid_spec=pltpu.PrefetchScalarGridSpec(
            num_scalar_prefetch=2,
            grid=(NB // 2,),
            in_specs=[smem(lambda g, ea, eb: (0, 0, 0)),
                      smem(lambda g, ea, eb: (2 * g + 1, 0, 0)),
                      smem(lambda g, ea, eb: (jnp.minimum(2 * g + 2, NB - 1), 0, 0)),
                      smem(lambda g, ea, eb: (2 * g, 0, 0)),
                      smem(lambda g, ea, eb: (2 * g + 1, 0, 0)),
                      smem(lambda g, ea, eb: (2 * g + 2, 0, 0)),
                      pl.BlockSpec(memory_space=pl.ANY)] + wspecs(0) + wspecs(1),
            out_specs=pl.BlockSpec(memory_space=pl.ANY),
            scratch_shapes=[pltpu.VMEM(((R + 2) * nslab, LANES), U32)] * 4
                           + [pltpu.SemaphoreType.DMA((2,)), pltpu.SemaphoreType.DMA((2,))]),
        out_shape=jax.ShapeDtypeStruct((TOP_K * T + R, nslab, LANES), U32),
        compiler_params=_params("arbitrary"),
        name="moe",
    )(jnp.concatenate([block_e[0::2], tail]), jnp.concatenate([block_e[1::2], tail]), tok, tok, tok,
      tgt_shift, tgt_shift, tgt_shift, h2p, w1, b1, w2, b2, w1, b1, w2, b2)


def _combine_call(x2, out4, gate, *, tm):
    T, D = x2.shape
    nt = T // tm
    nslab = out4.shape[1]
    out4 = out4.reshape(-1, LANES)
    trow = lambda w: pl.BlockSpec((tm, w), lambda i: (i, 0))
    return pl.pallas_call(
        _combine_kernel,
        grid=(nt,),
        in_specs=[trow(D)]
                 + [pl.BlockSpec((tm * nslab, LANES), lambda i, kk=kk: (kk * nt + i, 0)) for kk in range(TOP_K)]
                 + [trow(LANES)],
        out_specs=trow(D),
        out_shape=jax.ShapeDtypeStruct((T, D), F32),
        compiler_params=_params("arbitrary"),
        name="combine",
    )(x2, out4, out4, out4, out4, gate)


def kernel(x, mem, positions, g_attn_norm, w_in, w_pool, pool_scale, g_q_lora, w_uq, g_kv_lora, w_ukv, g_q_mla, g_k_mla, g_mem_norm, w_mem_kv, g_q_cross, g_k_cross, w_pool_out, w_mla_out, w_cross_out, w_o, g_ffn_norm, w_router, b_router, w_exp_in, b_exp_in, w_exp_out, b_exp_out):
    B, S, D = x.shape
    T = B * S
    N = T * TOP_K
    E = N_EXPERTS
    R = ROW_BLOCK
    NB = (N + E * (R - 1) + R - 1) // R
    NB += NB % 2
    tm = min(512, S)
    tq = min(1024, S)
    assert S % tm == 0 and T % tm == 0 and tm >= POOL_TAIL

    zpad = jnp.zeros((D, LANES - QK_ROPE), w_in.dtype)
    win = jnp.concatenate([w_in[:, :C_KPE + QK_ROPE], zpad, w_in[:, C_KPE + QK_ROPE:]], axis=1).astype(BF16)
    wuq = jnp.pad(w_uq.reshape(Q_LORA, MLA_HEADS, QK_HEAD), ((0, 0), (0, 0), (0, LANES - QK_HEAD)))
    wuq = wuq.reshape(Q_LORA, MLA_HEADS * LANES).astype(BF16)
    wukv = w_ukv.reshape(KV_LORA, MLA_HEADS, QK_NOPE + V_HEAD)
    wuk = jnp.pad(wukv[:, :, :QK_NOPE], ((0, 0), (0, 0), (0, LANES - QK_NOPE)))
    wuk = wuk.reshape(KV_LORA, MLA_HEADS * LANES).astype(BF16)
    wuv = wukv[:, :, QK_NOPE:].reshape(KV_LORA, MLA_HEADS * V_HEAD).astype(BF16)
    gq = jnp.pad(g_q_mla, (0, LANES - QK_HEAD)).reshape(1, LANES)
    gk = jnp.pad(g_k_mla, (0, LANES - QK_HEAD)).reshape(1, LANES)
    half = QK_ROPE // 2
    inv = ROPE_BASE ** (-jnp.arange(half, dtype=F32) / half)
    invf = jnp.zeros((LANES,), F32).at[QK_NOPE:QK_NOPE + half].set(inv).at[QK_NOPE + half:QK_HEAD].set(inv)
    invf = invf.reshape(1, LANES)
    wr = jnp.pad(w_router, ((0, 0), (0, LANES - E))).astype(BF16)
    br = jnp.concatenate([b_router, jnp.full((LANES - E,), NEG_BIG, F32)]).reshape(1, LANES)
    row = lambda v: v.reshape(1, -1)
    flat = lambda a: a.reshape(T, a.shape[-1])

    kc, vc = _memkv_call(mem, row(g_mem_norm), w_mem_kv.astype(BF16), row(g_k_cross))
    ypool, q, k, v, ocross, gates = _stage1_call(
        x, positions.reshape(B, S, 1), kc, vc, win, row(g_attn_norm), w_pool.astype(BF16),
        pool_scale.reshape(1, POOL_W), row(g_q_lora), wuq, row(g_kv_lora), wuk, wuv, gq, gk, invf,
        row(g_q_cross), tm=tm)
    omla = _mla_call(q, k, v, tq=tq)
    x2, h2p, idx, gate, rank, cnt = _stage3_call(
        flat(ypool), flat(omla), flat(ocross), flat(gates), flat(x), w_pool_out.astype(BF16),
        w_mla_out.astype(BF16), w_cross_out.astype(BF16), w_o.astype(BF16), row(g_ffn_norm), wr, br, tm=tm)

    counts = cnt[0, :E].astype(I32)
    padded = (counts + R - 1) // R * R
    pend = jnp.cumsum(padded)
    pstart = pend - padded
    dest = (pstart[idx[:, :TOP_K]] + rank[:, :TOP_K]).reshape(-1)
    row0 = jnp.arange(NB, dtype=I32) * R
    block_e = jnp.minimum(jnp.sum(row0[:, None] >= pend[None, :], axis=1), E - 1).astype(I32)
    src = jnp.full((NB * R,), -1, I32).at[dest].set(jnp.arange(N, dtype=I32), unique_indices=True,
                                                    mode='promise_in_bounds')
    dump = TOP_K * T + jnp.arange(NB * R, dtype=I32) % R
    tok = jnp.where(src >= 0, src // TOP_K, 0).reshape(NB, 1, R)
    tgt = jnp.where(src >= 0, (src % TOP_K) * T + src // TOP_K, dump)
    tgt_shift = jnp.concatenate([dump[:R], tgt]).reshape(NB + 1, 1, R)

    FF2 = w_exp_in.shape[-1]
    last_step = jnp.minimum((pend[-1] // R + 1) // 2, NB // 2 - 1)
    out4 = _moe_call(block_e, last_step, tok, tgt_shift, h2p.reshape(T, -1, LANES), w_exp_in.astype(BF16),
                     b_exp_in.reshape(E, 1, FF2), w_exp_out.astype(BF16), b_exp_out.reshape(E, 1, D))
    return _combine_call(x2, out4, gate, tm=tm).reshape(B, S, D)
```

```python
import functools

import jax
import jax.numpy as jnp
from jax import lax
from jax.experimental import pallas as pl
from jax.experimental.pallas import tpu as pltpu

F32 = jnp.float32
BF16 = jnp.bfloat16
U32 = jnp.uint32
I32 = jnp.int32

EPS = 1e-6
ROPE_BASE = 10000.0
LANES = 128
SUBLANES = 8
MXU_COLS = 256
POOL_WINDOWS = (2, 4, 8, 16)
POOL_TAIL = 16
POOL_W = 512
MLA_HEADS = 8
QK_NOPE = 64
QK_ROPE = 32
QK_HEAD = QK_NOPE + QK_ROPE
V_HEAD = 64
Q_LORA = 384
KV_LORA = 256
CROSS_HEADS = 4
CROSS_HEAD_DIM = 128
CROSS_W = 512
N_EXPERTS = 32
TOP_K = 4
SWIGLU_LIMIT = 7.0
SWIGLU_ALPHA = 1.702
ROW_BLOCK = 512
LOG2E = 1.4426950408889634
NEG_BIG = -1e30

C_POOL = 0
C_QL = C_POOL + POOL_W
C_KVL = C_QL + Q_LORA
C_KPE = C_KVL + KV_LORA
C_XQ = C_KPE + LANES
C_GATE = C_XQ + CROSS_W

VMEM_LIMIT = 56 * 1024 * 1024


def _dot(a, b):
    return jnp.dot(a, b, preferred_element_type=F32)


def _dot_nt(a, b):
    return lax.dot_general(a, b, (((1,), (1,)), ((), ())), preferred_element_type=F32)


def _rms(xf, g, n=None):
    n = xf.shape[-1] if n is None else n
    ms = jnp.sum(xf * xf, axis=-1, keepdims=True) * (1.0 / n)
    return xf * lax.rsqrt(ms + EPS) * g


def _sigmoid(x):
    return 0.5 * jnp.tanh(0.5 * x) + 0.5


def _pack_bf16_pair(lo, hi):
    lo_b = lax.bitcast_convert_type(lo.astype(BF16).astype(F32), U32)
    hi_b = lax.bitcast_convert_type(hi.astype(BF16).astype(F32), U32)
    return (lo_b >> 16) | (hi_b & jnp.uint32(0xFFFF0000))


def _unpack_bf16_pair(w):
    lo = lax.bitcast_convert_type(w << 16, F32)
    hi = lax.bitcast_convert_type(w & jnp.uint32(0xFFFF0000), F32)
    return lo, hi


def _memkv_kernel(mem_ref, g_ref, w_ref, gk_ref, k_ref, v_ref):
    h = _rms(mem_ref[0], g_ref[...]).astype(BF16)
    kv = _dot(h, w_ref[...])
    for c in range(CROSS_HEADS):
        sl = slice(c * CROSS_HEAD_DIM, (c + 1) * CROSS_HEAD_DIM)
        k_ref[0, :, sl] = _rms(kv[:, sl], gk_ref[...]).astype(BF16)
    v_ref[0] = kv[:, CROSS_W:].astype(BF16)


def _stage1_kernel(x_ref, pos_ref, kc_ref, vc_ref, win_ref, gattn_ref, wpool_ref, pscale_ref,
                   gql_ref, wuq_ref, gkvl_ref, wuk_ref, wuvt_ref, gq_ref, gk_ref, invf_ref, gqx_ref,
                   ypool_ref, q_ref, k_ref, v_ref, oc_ref, gates_ref, tail_ref, *, tm):
    i = pl.program_id(1)

    @pl.when(i == 0)
    def _():
        tail_ref[...] = jnp.zeros_like(tail_ref)

    h = _rms(x_ref[0], gattn_ref[...]).astype(BF16)

    u = _dot(h, win_ref[:, C_POOL:C_QL])
    ext = jnp.concatenate([tail_ref[...], u], axis=0)
    tail_ref[...] = u[tm - POOL_TAIL:, :]
    tpos = (i * tm + lax.broadcasted_iota(I32, (tm, 1), 0) + 1).astype(F32)
    run = ext
    shift = 1
    for g, win in enumerate(POOL_WINDOWS):
        while shift < win:
            run = run + pltpu.roll(run, shift, 0)
            shift *= 2
        sl = slice(g * LANES, (g + 1) * LANES)
        cnt = jnp.minimum(tpos, float(win))
        mixed = run[POOL_TAIL:, sl] / cnt - u[:, sl]
        y = _dot(mixed.astype(BF16), wpool_ref[g]) * pscale_ref[:, sl]
        ypool_ref[0, :, sl] = y.astype(BF16)

    half = QK_ROPE // 2
    ang = invf_ref[...] * pos_ref[0].astype(F32)
    ct, st = jnp.cos(ang), jnp.sin(ang)
    const = lambda n, v: jnp.full((n, tm), v, F32)
    cosf = jnp.concatenate([const(QK_NOPE, 1.0), ct, ct, const(LANES - QK_HEAD, 1.0)], axis=0).T
    sin_a = jnp.concatenate([const(QK_NOPE, 0.0), -st, const(LANES - QK_NOPE - half, 0.0)], axis=0).T
    sin_b = jnp.concatenate([const(QK_NOPE + half, 0.0), st, const(LANES - QK_HEAD, 0.0)], axis=0).T

    def rope(t):
        return t * cosf + pltpu.roll(t, LANES - half, 1) * sin_a + pltpu.roll(t, half, 1) * sin_b

    cq = _dot(h, win_ref[:, C_QL:C_KVL])
    qf = _dot(_rms(cq, gql_ref[...]).astype(BF16), wuq_ref[...])
    qscale = QK_HEAD ** -0.5 * LOG2E
    for hh in range(MLA_HEADS):
        sl = slice(hh * LANES, (hh + 1) * LANES)
        qn = _rms(qf[:, sl], gq_ref[...], QK_HEAD)
        q_ref[0, :, sl] = (rope(qn) * qscale).astype(BF16)

    ckv = _dot(h, win_ref[:, C_KVL:C_KPE])
    ckvn = _rms(ckv, gkvl_ref[...]).astype(BF16)
    kf = _dot(ckvn, wuk_ref[...])
    kpe = pltpu.roll(_dot(h, win_ref[:, C_KPE:C_XQ]), QK_NOPE, 1)
    for hh in range(MLA_HEADS):
        sl = slice(hh * LANES, (hh + 1) * LANES)
        kn = _rms(kf[:, sl] + kpe, gk_ref[...], QK_HEAD)
        k_ref[0, :, sl] = rope(kn).astype(BF16)
    v_ref[0, 0] = _dot_nt(wuvt_ref[...], ckvn).astype(BF16)

    qx = _dot(h, win_ref[:, C_XQ:C_GATE])
    xscale = CROSS_HEAD_DIM ** -0.5 * LOG2E
    for c in range(CROSS_HEADS):
        sl = slice(c * CROSS_HEAD_DIM, (c + 1) * CROSS_HEAD_DIM)
        qc = (_rms(qx[:, sl], gqx_ref[...]) * xscale).astype(BF16)
        s = _dot_nt(qc, kc_ref[0, :, sl])
        p = jnp.exp2(s - jnp.max(s, axis=-1, keepdims=True))
        o = _dot(p.astype(BF16), vc_ref[0, :, sl]) / jnp.sum(p, axis=-1, keepdims=True)
        oc_ref[0, :, sl] = o.astype(BF16)

    d = x_ref.shape[-1]
    for g in range(3):
        zg = _dot(h, win_ref[:, C_GATE + g * d:C_GATE + (g + 1) * d])
        gates_ref[0, :, g * d:(g + 1) * d] = _sigmoid(zg).astype(BF16)


def _mla_kernel(q_ref, k_ref, vt_ref, o_ref, *, tq):
    i = pl.program_id(2)
    qs = [q_ref[0, :, hh * LANES:(hh + 1) * LANES] for hh in range(2)]

    def step(j, carry, masked):
        off = pl.multiple_of(j * tq, tq)
        new = []
        for hh in range(2):
            m, l, acc = carry[hh]
            kt = k_ref[0, pl.ds(off, tq), hh * LANES:(hh + 1) * LANES]
            st = _dot_nt(kt, qs[hh])
            if masked:
                key = lax.broadcasted_iota(I32, st.shape, 0)
                qry = lax.broadcasted_iota(I32, st.shape, 1)
                st = jnp.where(key <= qry, st, NEG_BIG)
            m_new = jnp.maximum(m, jnp.max(st, axis=0, keepdims=True))
            a = jnp.exp2(m - m_new)
            p = jnp.exp2(st - m_new)
            l = a * l + jnp.sum(p, axis=0, keepdims=True)
            vt = vt_ref[0, j, hh * V_HEAD:(hh + 1) * V_HEAD, :]
            acc = a * acc + _dot(vt, p.astype(BF16))
            new.append((m_new, l, acc))
        return tuple(new)

    init = (jnp.full((1, tq), NEG_BIG, F32), jnp.zeros((1, tq), F32), jnp.zeros((V_HEAD, tq), F32))
    carry = lax.fori_loop(0, i, functools.partial(step, masked=False), (init, init))
    (_, l0, acc0), (_, l1, acc1) = step(i, carry, True)
    o_ref[0] = jnp.concatenate([acc0 / l0, acc1 / l1], axis=0).T.astype(BF16)


def _stage3_kernel(yp_ref, om_ref, oc_ref, gates_ref, x_ref, wpo_ref, wmo_ref, wco_ref, wo_ref,
                   gffn_ref, wr_ref, br_ref,
                   x2_ref, h2p_ref, idx_ref, gate_ref, rank_ref, cnt_ref, carry_ref, *, tm):
    i = pl.program_id(0)

    @pl.when(i == 0)
    def _():
        carry_ref[...] = jnp.zeros_like(carry_ref)

    d = x_ref.shape[-1]
    merged = (gates_ref[:, 0:d].astype(F32) * _dot(yp_ref[...], wpo_ref[...])
              + gates_ref[:, d:2 * d].astype(F32) * _dot(om_ref[...], wmo_ref[...])
              + gates_ref[:, 2 * d:3 * d].astype(F32) * _dot(oc_ref[...], wco_ref[...]))
    x2 = x_ref[...] + _dot(merged.astype(BF16), wo_ref[...])
    x2_ref[...] = x2
    h2 = _rms(x2, gffn_ref[...])
    h2p = _pack_bf16_pair(h2[:, :d // 2], h2[:, d // 2:])
    nslab = h2p_ref.shape[0] // tm
    for j in range(nslab):
        h2p_ref[pl.ds(j, tm, stride=nslab), :] = h2p[:, j * LANES:(j + 1) * LANES]

    logits = _dot(h2.astype(BF16), wr_ref[...]) + br_ref[...]
    lanef = lax.broadcasted_iota(I32, (tm, LANES), 1).astype(F32)
    rem = logits
    tops, hots = [], []
    for _ in range(TOP_K):
        m = jnp.max(rem, axis=-1, keepdims=True)
        ix = jnp.min(jnp.where(rem == m, lanef, float(LANES)), axis=-1, keepdims=True)
        hot = lanef == ix
        tops.append((m, ix))
        hots.append(hot)
        rem = jnp.where(hot, NEG_BIG * 2, rem)
    es = [jnp.exp(m - tops[0][0]) for m, _ in tops]
    denom = es[0] + es[1] + es[2] + es[3]

    onehot = jnp.zeros((tm, LANES), F32)
    for hot in hots:
        onehot = onehot + hot.astype(F32)
    r = lax.broadcasted_iota(I32, (tm, tm), 0)
    c = lax.broadcasted_iota(I32, (tm, tm), 1)
    tri = (c < r).astype(BF16)
    before = _dot(tri, onehot.astype(BF16)) + carry_ref[...]
    carry_new = carry_ref[...] + jnp.sum(onehot, axis=0, keepdims=True)
    carry_ref[...] = carry_new
    cnt_ref[...] = carry_new

    idx_out = jnp.zeros((tm, LANES), F32)
    gate_out = jnp.zeros((tm, LANES), F32)
    rank_out = jnp.zeros((tm, LANES), F32)
    for k in range(TOP_K):
        rk = jnp.sum(jnp.where(hots[k], before, 0.0), axis=-1, keepdims=True)
        idx_out = jnp.where(lanef == float(k), tops[k][1], idx_out)
        gate_out = jnp.where(lanef == float(k), es[k] / denom, gate_out)
        rank_out = jnp.where(lanef == float(k), rk, rank_out)
    idx_ref[...] = idx_out.astype(I32)
    gate_ref[...] = gate_out
    rank_ref[...] = rank_out.astype(I32)


def _moe_kernel(bea_ref, beb_ref, tok0_ref, tok1_ref, tok2_ref, tgtm_ref, tgt0_ref, tgt1_ref, h2p_hbm,
                w1a_ref, b1a_ref, w2a_ref, b2a_ref, w1b_ref, b1b_ref, w2b_ref, b2b_ref,
                out_hbm, xa, xb, oa, ob, gsem, ssem):
    g = pl.program_id(0)
    ng = pl.num_programs(0)
    nslab = h2p_hbm.shape[1]
    rows = xa.shape[0] // nslab - 2
    half = nslab * LANES
    ff = w2a_ref.shape[1]
    cw = MXU_COLS
    n1, n2 = 2 * ff // cw, 2 * half // cw
    per = -(-rows // (n1 + n2))

    def slab(buf, r):
        return buf.at[pl.ds(r * nslab, nslab), :]

    def wait_gather(sem):
        pltpu.make_async_copy(h2p_hbm.at[pl.ds(0, rows)], h2p_hbm.at[pl.ds(0, rows)], sem).wait()

    def wait_scatter(sem):
        pltpu.make_async_copy(out_hbm.at[pl.ds(0, rows)], out_hbm.at[pl.ds(0, rows)], sem).wait()

    def expert(xbuf, w1_ref, b1_ref, w2_ref, b2_ref, obuf, tok_ref, xnext, gs, oprev, tgt_ref, ss):
        def issue(c):
            for r in range(c * per, min((c + 1) * per, rows)):
                pltpu.make_async_copy(h2p_hbm.at[tok_ref[0, 0, r]], slab(xnext, r), gs).start()
                pltpu.make_async_copy(slab(oprev, r), out_hbm.at[tgt_ref[0, 0, r]], ss).start()
            spare_a = pl.multiple_of(bea_ref[ng], SUBLANES)
            spare_b = pl.multiple_of(beb_ref[ng], SUBLANES)
            oprev[pl.ds(spare_a, SUBLANES), :] = xnext[pl.ds(spare_a, SUBLANES), :]
            z = lax.bitcast_convert_type((oprev[pl.ds(spare_b, SUBLANES), :] >> 16) >> 16, F32)[0:1, :]
            return jnp.concatenate([z] * (cw // LANES), axis=1)

        xw = jnp.concatenate([xbuf[pl.ds(j, rows, stride=nslab), :] for j in range(nslab)], axis=1)
        lo, hi = _unpack_bf16_pair(xw)
        lo, hi = lo.astype(BF16), hi.astype(BF16)
        hm = []
        for c in range(n1):
            cs = slice(c * cw, (c + 1) * cw)
            hm.append(_dot(lo, w1_ref[0, :half, cs]) + _dot(hi, w1_ref[0, half:, cs]) + (b1_ref[0, :, cs] + issue(c)))
        acts = []
        for c in range(n1 // 2):
            glu = jnp.minimum(hm[c], SWIGLU_LIMIT)
            lin = jnp.clip(hm[n1 // 2 + c], -SWIGLU_LIMIT, SWIGLU_LIMIT)
            acts.append((glu * _sigmoid(SWIGLU_ALPHA * glu) * (lin + 1.0)).astype(BF16))
        act = jnp.concatenate(acts, axis=1)
        o = []
        for c in range(n2):
            cs = slice(c * cw, (c + 1) * cw)
            o.append(_dot(act, w2_ref[0, :, cs]) + (b2_ref[0, :, cs] + issue(n1 + c)))
        ow = _pack_bf16_pair(jnp.concatenate(o[:n2 // 2], axis=1), jnp.concatenate(o[n2 // 2:], axis=1))
        for j in range(nslab):
            obuf[pl.ds(j, rows, stride=nslab), :] = ow[:, j * LANES:(j + 1) * LANES]

    @pl.when(g == 0)
    def _():
        xa[...] = jnp.zeros_like(xa)
        xb[...] = jnp.zeros_like(xb)
        oa[...] = jnp.zeros_like(oa)
        ob[...] = jnp.zeros_like(ob)

        def gather0(r, c):
            pltpu.make_async_copy(h2p_hbm.at[tok0_ref[0, 0, r]], slab(xa, r), gsem.at[0]).start()
            return c

        lax.fori_loop(0, rows, gather0, 0)

    last = bea_ref[ng + 1]

    @pl.when(g <= last)
    def _():
        wait_gather(gsem.at[0])

        @pl.when(g >= 1)
        def _():
            wait_scatter(ssem.at[0])

        expert(xa, w1a_ref, b1a_ref, w2a_ref, b2a_ref, oa, tok1_ref, xb, gsem.at[1], ob, tgtm_ref, ssem.at[1])

        wait_gather(gsem.at[1])
        wait_scatter(ssem.at[1])

        expert(xb, w1b_ref, b1b_ref, w2b_ref, b2b_ref, ob, tok2_ref, xa, gsem.at[0], oa, tgt0_ref, ssem.at[0])

    @pl.when(g == last)
    def _():
        def scatter_last(r, c):
            pltpu.make_async_copy(slab(ob, r), out_hbm.at[tgt1_ref[0, 0, r]], ssem.at[1]).start()
            return c

        wait_scatter(ssem.at[0])
        lax.fori_loop(0, rows, scatter_last, 0)
        wait_scatter(ssem.at[1])
        wait_gather(gsem.at[0])


def _combine_kernel(x2_ref, o0_ref, o1_ref, o2_ref, o3_ref, gate_ref, out_ref):
    tm = x2_ref.shape[0]
    nslab = o0_ref.shape[0] // tm
    half = nslab * LANES
    gates = [gate_ref[:, k:k + 1] for k in range(TOP_K)]
    for j in range(nslab):
        lo_sl = slice(j * LANES, (j + 1) * LANES)
        hi_sl = slice(half + j * LANES, half + (j + 1) * LANES)
        acc_lo = x2_ref[:, lo_sl]
        acc_hi = x2_ref[:, hi_sl]
        for k, o_ref in enumerate((o0_ref, o1_ref, o2_ref, o3_ref)):
            lo, hi = _unpack_bf16_pair(o_ref[pl.ds(j, tm, stride=nslab), :])
            acc_lo = acc_lo + gates[k] * lo
            acc_hi = acc_hi + gates[k] * hi
        out_ref[:, lo_sl] = acc_lo
        out_ref[:, hi_sl] = acc_hi


def _full(shape):
    return pl.BlockSpec(shape, lambda *_: (0,) * len(shape))


def _params(*semantics):
    return pltpu.CompilerParams(dimension_semantics=semantics, vmem_limit_bytes=VMEM_LIMIT)


def _memkv_call(mem, g_mem, w_mem_kv, g_k_cross):
    B, M, D = mem.shape
    return pl.pallas_call(
        _memkv_kernel,
        grid=(B,),
        in_specs=[pl.BlockSpec((1, M, D), lambda b: (b, 0, 0)), _full((1, D)), _full((D, 2 * CROSS_W)),
                  _full((1, CROSS_HEAD_DIM))],
        out_specs=[pl.BlockSpec((1, M, CROSS_W), lambda b: (b, 0, 0))] * 2,
        out_shape=[jax.ShapeDtypeStruct((B, M, CROSS_W), BF16)] * 2,
        compiler_params=_params("arbitrary"),
        name="memkv",
    )(mem, g_mem, w_mem_kv, g_k_cross)


def _stage1_call(x, pos, kc, vc, win, g_attn, wpool, pscale, gql, wuq, gkvl, wuk, wuvt, gq, gk, invf, gqx, *, tm, tkv):
    B, S, D = x.shape
    M = kc.shape[1]
    per = tkv // tm
    tile = lambda w: pl.BlockSpec((1, tm, w), lambda b, i: (b, i, 0))
    widths = (POOL_W, MLA_HEADS * LANES, MLA_HEADS * LANES, None, CROSS_W, 3 * D)
    vt_spec = pl.BlockSpec((1, 1, MLA_HEADS * V_HEAD, tm), lambda b, i: (b, i // per, 0, i % per))
    vt_shape = jax.ShapeDtypeStruct((B, S // tkv, MLA_HEADS * V_HEAD, tkv), BF16)
    return pl.pallas_call(
        functools.partial(_stage1_kernel, tm=tm),
        grid=(B, S // tm),
        in_specs=[tile(D), pl.BlockSpec((1, 1, tm), lambda b, i: (b, 0, i)),
                  pl.BlockSpec((1, M, CROSS_W), lambda b, i: (b, 0, 0)),
                  pl.BlockSpec((1, M, CROSS_W), lambda b, i: (b, 0, 0)),
                  _full(win.shape), _full((1, D)), _full(wpool.shape), _full((1, POOL_W)),
                  _full((1, Q_LORA)), _full(wuq.shape), _full((1, KV_LORA)), _full(wuk.shape), _full(wuvt.shape),
                  _full((1, LANES)), _full((1, LANES)), _full(invf.shape), _full((1, CROSS_HEAD_DIM))],
        out_specs=[vt_spec if w is None else tile(w) for w in widths],
        out_shape=[vt_shape if w is None else jax.ShapeDtypeStruct((B, S, w), BF16) for w in widths],
        scratch_shapes=[pltpu.VMEM((POOL_TAIL, POOL_W), F32)],
        compiler_params=_params("arbitrary", "arbitrary"),
        name="stage1",
    )(x, pos, kc, vc, win, g_attn, wpool, pscale, gql, wuq, gkvl, wuk, wuvt, gq, gk, invf, gqx)


def _mla_call(q, k, vt, *, tq):
    B, S, _ = q.shape
    nkt = S // tq
    return pl.pallas_call(
        functools.partial(_mla_kernel, tq=tq),
        grid=(B, MLA_HEADS // 2, S // tq),
        in_specs=[pl.BlockSpec((1, tq, 2 * LANES), lambda b, hp, i: (b, i, hp)),
                  pl.BlockSpec((1, S, 2 * LANES), lambda b, hp, i: (b, 0, hp)),
                  pl.BlockSpec((1, nkt, 2 * V_HEAD, tq), lambda b, hp, i: (b, 0, hp, 0))],
        out_specs=pl.BlockSpec((1, tq, 2 * V_HEAD), lambda b, hp, i: (b, i, hp)),
        out_shape=jax.ShapeDtypeStruct((B, S, MLA_HEADS * V_HEAD), BF16),
        compiler_params=_params("arbitrary", "arbitrary", "arbitrary"),
        name="mla",
    )(q, k, vt)


def _stage3_call(ypool, omla, ocross, gates, x, wpo, wmo, wco, wo, g_ffn, wr, br, *, tm):
    T, D = x.shape
    trow = lambda w: pl.BlockSpec((tm, w), lambda i: (i, 0))
    return pl.pallas_call(
        functools.partial(_stage3_kernel, tm=tm),
        grid=(T // tm,),
        in_specs=[trow(POOL_W), trow(MLA_HEADS * V_HEAD), trow(CROSS_W), trow(3 * D), trow(D),
                  _full((POOL_W, D)), _full((MLA_HEADS * V_HEAD, D)), _full((CROSS_W, D)), _full((D, D)),
                  _full((1, D)), _full((D, LANES)), _full((1, LANES))],
        out_specs=[trow(D), pl.BlockSpec((tm * (D // 2 // LANES), LANES), lambda i: (i, 0)),
                   trow(LANES), trow(LANES), trow(LANES), _full((1, LANES))],
        out_shape=[jax.ShapeDtypeStruct((T, D), F32), jax.ShapeDtypeStruct((T * (D // 2 // LANES), LANES), U32),
                   jax.ShapeDtypeStruct((T, LANES), I32), jax.ShapeDtypeStruct((T, LANES), F32),
                   jax.ShapeDtypeStruct((T, LANES), I32), jax.ShapeDtypeStruct((1, LANES), F32)],
        scratch_shapes=[pltpu.VMEM((1, LANES), F32)],
        compiler_params=_params("arbitrary"),
        name="stage3",
    )(ypool, omla, ocross, gates, x, wpo, wmo, wco, wo, g_ffn, wr, br)


def _moe_call(block_e, last_step, tok, tgt_shift, h2p, w1, b1, w2, b2):
    NB, _, R = tok.shape
    T, nslab, _ = h2p.shape
    tail = jnp.stack([jnp.int32(R * nslab), last_step.astype(I32)])
    _, D, FF2 = w1.shape
    smem = lambda imap: pl.BlockSpec((1, 1, R), imap, memory_space=pltpu.SMEM)

    def wspecs(which):
        imap = lambda g, ea, eb: ((ea, eb)[which][g], 0, 0)
        return [pl.BlockSpec((1, D, FF2), imap), pl.BlockSpec((1, 1, FF2), imap),
                pl.BlockSpec((1, FF2 // 2, D), imap), pl.BlockSpec((1, 1, D), imap)]

    return pl.pallas_call(
        _moe_kernel,
        grid_spec=pltpu.PrefetchScalarGridSpec(
            num_scalar_prefetch=2,
            grid=(NB // 2,),
            in_specs=[smem(lambda g, ea, eb: (0, 0, 0)),
                      smem(lambda g, ea, eb: (2 * g + 1, 0, 0)),
                      smem(lambda g, ea, eb: (jnp.minimum(2 * g + 2, NB - 1), 0, 0)),
                      smem(lambda g, ea, eb: (2 * g, 0, 0)),
                      smem(lambda g, ea, eb: (2 * g + 1, 0, 0)),
                      smem(lambda g, ea, eb: (2 * g + 2, 0, 0)),
                      pl.BlockSpec(memory_space=pl.ANY)] + wspecs(0) + wspecs(1),
            out_specs=pl.BlockSpec(memory_space=pl.ANY),
            scratch_shapes=[pltpu.VMEM(((R + 2) * nslab, LANES), U32)] * 4
                           + [pltpu.SemaphoreType.DMA((2,)), pltpu.SemaphoreType.DMA((2,))]),
        out_shape=jax.ShapeDtypeStruct((TOP_K * T + R, nslab, LANES), U32),
        compiler_params=_params("arbitrary"),
        name="moe",
    )(jnp.concatenate([block_e[0::2], tail]), jnp.concatenate([block_e[1::2], tail]), tok, tok, tok,
      tgt_shift, tgt_shift, tgt_shift, h2p, w1, b1, w2, b2, w1, b1, w2, b2)


def _combine_call(x2, out4, gate, *, tm):
    T, D = x2.shape
    nt = T // tm
    nslab = out4.shape[1]
    out4 = out4.reshape(-1, LANES)
    trow = lambda w: pl.BlockSpec((tm, w), lambda i: (i, 0))
    return pl.pallas_call(
        _combine_kernel,
        grid=(nt,),
        in_specs=[trow(D)]
                 + [pl.BlockSpec((tm * nslab, LANES), lambda i, kk=kk: (kk * nt + i, 0)) for kk in range(TOP_K)]
                 + [trow(LANES)],
        out_specs=trow(D),
        out_shape=jax.ShapeDtypeStruct((T, D), F32),
        compiler_params=_params("arbitrary"),
        name="combine",
    )(x2, out4, out4, out4, out4, gate)


def kernel(x, mem, positions, g_attn_norm, w_in, w_pool, pool_scale, g_q_lora, w_uq, g_kv_lora, w_ukv, g_q_mla, g_k_mla, g_mem_norm, w_mem_kv, g_q_cross, g_k_cross, w_pool_out, w_mla_out, w_cross_out, w_o, g_ffn_norm, w_router, b_router, w_exp_in, b_exp_in, w_exp_out, b_exp_out):
    B, S, D = x.shape
    T = B * S
    N = T * TOP_K
    E = N_EXPERTS
    R = ROW_BLOCK
    NB = (N + E * (R - 1) + R - 1) // R
    NB += NB % 2
    tm = min(512, S)
    tq = min(1024, S)
    assert S % tm == 0 and T % tm == 0 and tm >= POOL_TAIL

    zpad = jnp.zeros((D, LANES - QK_ROPE), w_in.dtype)
    win = jnp.concatenate([w_in[:, :C_KPE + QK_ROPE], zpad, w_in[:, C_KPE + QK_ROPE:]], axis=1).astype(BF16)
    wuq = jnp.pad(w_uq.reshape(Q_LORA, MLA_HEADS, QK_HEAD), ((0, 0), (0, 0), (0, LANES - QK_HEAD)))
    wuq = wuq.reshape(Q_LORA, MLA_HEADS * LANES).astype(BF16)
    wukv = w_ukv.reshape(KV_LORA, MLA_HEADS, QK_NOPE + V_HEAD)
    wuk = jnp.pad(wukv[:, :, :QK_NOPE], ((0, 0), (0, 0), (0, LANES - QK_NOPE)))
    wuk = wuk.reshape(KV_LORA, MLA_HEADS * LANES).astype(BF16)
    wuvt = wukv[:, :, QK_NOPE:].reshape(KV_LORA, MLA_HEADS * V_HEAD).T.astype(BF16)
    gq = jnp.pad(g_q_mla, (0, LANES - QK_HEAD)).reshape(1, LANES)
    gk = jnp.pad(g_k_mla, (0, LANES - QK_HEAD)).reshape(1, LANES)
    half = QK_ROPE // 2
    inv = ROPE_BASE ** (-jnp.arange(half, dtype=F32) / half)
    invf = inv.reshape(half, 1)
    wr = jnp.pad(w_router, ((0, 0), (0, LANES - E))).astype(BF16)
    br = jnp.concatenate([b_router, jnp.full((LANES - E,), NEG_BIG, F32)]).reshape(1, LANES)
    row = lambda v: v.reshape(1, -1)
    flat = lambda a: a.reshape(T, a.shape[-1])

    kc, vc = _memkv_call(mem, row(g_mem_norm), w_mem_kv.astype(BF16), row(g_k_cross))
    ypool, q, k, v, ocross, gates = _stage1_call(
        x, positions.reshape(B, 1, S), kc, vc, win, row(g_attn_norm), w_pool.astype(BF16),
        pool_scale.reshape(1, POOL_W), row(g_q_lora), wuq, row(g_kv_lora), wuk, wuvt, gq, gk, invf,
        row(g_q_cross), tm=tm, tkv=tq)
    omla = _mla_call(q, k, v, tq=tq)
    x2, h2p, idx, gate, rank, cnt = _stage3_call(
        flat(ypool), flat(omla), flat(ocross), flat(gates), flat(x), w_pool_out.astype(BF16),
        w_mla_out.astype(BF16), w_cross_out.astype(BF16), w_o.astype(BF16), row(g_ffn_norm), wr, br, tm=tm)

    counts = cnt[0, :E].astype(I32)
    padded = (counts + R - 1) // R * R
    pend = jnp.cumsum(padded)
    pstart = pend - padded
    dest = (pstart[idx[:, :TOP_K]] + rank[:, :TOP_K]).reshape(-1)
    row0 = jnp.arange(NB, dtype=I32) * R
    block_e = jnp.minimum(jnp.sum(row0[:, None] >= pend[None, :], axis=1), E - 1).astype(I32)
    src = jnp.full((NB * R,), -1, I32).at[dest].set(jnp.arange(N, dtype=I32), unique_indices=True,
                                                    mode='promise_in_bounds')
    dump = TOP_K * T + jnp.arange(NB * R, dtype=I32) % R
    tok = jnp.where(src >= 0, src // TOP_K, 0).reshape(NB, 1, R)
    tgt = jnp.where(src >= 0, (src % TOP_K) * T + src // TOP_K, dump)
    tgt_shift = jnp.concatenate([dump[:R], tgt]).reshape(NB + 1, 1, R)

    FF2 = w_exp_in.shape[-1]
    last_step = jnp.minimum((pend[-1] // R + 1) // 2, NB // 2 - 1)
    out4 = _moe_call(block_e, last_step, tok, tgt_shift, h2p.reshape(T, -1, LANES), w_exp_in.astype(BF16),
                     b_exp_in.reshape(E, 1, FF2), w_exp_out.astype(BF16), b_exp_out.reshape(E, 1, D))
    return _combine_call(x2, out4, gate, tm=tm).reshape(B, S, D)
```
